```python
import jax, jax.numpy as jnp
from jax import lax
import numpy as np

D_MODEL = 4096
BATCH = 8
SEQ = 2048
DEPTH = 2
DEC_BATCH = 8
DEC_SEQ = 16
PAST_LEN = 2048

CHUNK = 64
D_MIX = D_MODEL
D_A = D_MIX // 2
A_GROUPS = 8
A_GD = D_A // A_GROUPS
A_BLOCK = 128
D_B = D_MIX - D_A
HEAD_DIM = 64
N_Q = D_B // HEAD_DIM
N_KV = 4
REP = N_Q // N_KV
WINDOW = 128
WIN_CHUNKS = WINDOW // CHUNK
BAND = (WIN_CHUNKS + 1) * CHUNK
D_FF = 256 * ((8 * D_MODEL // 3 + 255) // 256)
CONV_W = 3
KV_W = N_KV * HEAD_DIM
IN_COLS = 2 * D_A + D_B + 2 * KV_W
EPS = 1e-6
NEG = -1e30

kernel_name = "hymba_style_streaming_chunkmlp_swa_convffn"


def rms_norm(x, g):
    xf = x.astype(jnp.float32)
    y = xf * lax.rsqrt(jnp.mean(xf * xf, axis=-1, keepdims=True) + EPS)
    return (y * g.astype(jnp.float32)).astype(x.dtype)


def softmax_with_sink(s, sink):
    sk = sink.astype(jnp.float32).reshape(N_KV, REP, 1, 1)
    m = jnp.maximum(jnp.max(s, axis=-1, keepdims=True), sk)
    e = jnp.exp(s - m)
    return e / (jnp.sum(e, axis=-1, keepdims=True) + jnp.exp(sk - m))


def chunk_mlp_mixer(u, v, w_s, b_s):
    B, L, _ = u.shape
    n = min(L, A_BLOCK)
    nb = L // n
    vf = v.astype(jnp.float32).reshape(B, nb, n, A_GROUPS, A_GD)
    vn = vf * lax.rsqrt(jnp.mean(vf * vf, axis=-1, keepdims=True) + EPS)
    i = jnp.arange(n)
    mask = (i[None, :] // CHUNK) <= (i[:, None] // CHUNK)
    w = jnp.where(mask[None], w_s[:, :n, :n].astype(jnp.float32), 0.0)
    bias = b_s[:, :n].astype(jnp.float32).T[None, None, :, :, None]
    s = jnp.einsum('gij,bnjgd->bnigd', w, vn) + bias
    out = u.astype(jnp.float32).reshape(B, nb, n, A_GROUPS, A_GD) * s
    return out.reshape(B, L, D_A).astype(u.dtype), vn.reshape(B, L, D_A).astype(v.dtype)


def window_attn_prompt(q, k, v, sink):
    B, S = q.shape[:2]
    nc = S // CHUNK
    pad = WIN_CHUNKS * CHUNK
    qb = q.reshape(B, nc, CHUNK, N_KV, REP, HEAD_DIM).astype(jnp.float32)

    def band(t):
        tp = jnp.pad(t, ((0, 0), (pad, 0), (0, 0), (0, 0)))
        tp = tp.reshape(B, nc + WIN_CHUNKS, CHUNK, N_KV, HEAD_DIM)
        return jnp.concatenate([tp[:, j:j + nc] for j in range(WIN_CHUNKS + 1)], axis=2)

    kb = band(k).astype(jnp.float32)
    vb = band(v).astype(jnp.float32)
    s = jnp.einsum('bnqgrd,bnkgd->bngrqk', qb, kb) * (HEAD_DIM ** -0.5)
    key_pos = jnp.arange(nc)[:, None] * CHUNK + jnp.arange(BAND)[None, :] - pad
    s = jnp.where((key_pos >= 0)[None, :, None, None, None, :], s, NEG)
    p = softmax_with_sink(s, sink)
    o = jnp.einsum('bngrqk,bnkgd->bnqgrd', p, vb)
    return o.reshape(B, S, D_B).astype(q.dtype)


def window_attn_sample(q, k, v, ck, cv, sink):
    B, T = q.shape[:2]
    kk = jnp.concatenate([ck.astype(k.dtype), k], axis=1).astype(jnp.float32)
    vv = jnp.concatenate([cv.astype(v.dtype), v], axis=1).astype(jnp.float32)
    qb = q.reshape(B, T, N_KV, REP, HEAD_DIM).astype(jnp.float32)
    s = jnp.einsum('btgrd,bkgd->bgrtk', qb, kk) * (HEAD_DIM ** -0.5)
    p = softmax_with_sink(s, sink)
    o = jnp.einsum('bgrtk,bkgd->btgrd', p, vv)
    return o.reshape(B, T, D_B).astype(q.dtype)


def conv_ffn(h, w_up, conv_w, conv_b, w_down, conv_state):
    L = h.shape[1]
    up = h @ w_up
    a, b = up[..., :D_FF], up[..., D_FF:]
    ap = jnp.concatenate([conv_state.astype(a.dtype), a], axis=1)
    conv = sum(conv_w[j] * ap[:, j:j + L] for j in range(CONV_W)) + conv_b
    y = (jax.nn.silu(conv) * b) @ w_down
    return y, ap[:, -(CONV_W - 1):]


def _trunk(x, c, params, cache):
    (w_mod, b_mod, g_pre_mix, g_post_mix, w_in, sgu_w, sgu_b, attn_sink, g_out_a, g_out_b,
     w_out, g_pre_ffn, g_post_ffn, w_up, conv_w, conv_b, w_down) = params
    B, L, _ = x.shape
    sample = cache is not None
    ks, vs, convs, sgus = [], [], [], []
    c_act = jax.nn.silu(c)
    splits = [D_A, 2 * D_A, 2 * D_A + D_B, 2 * D_A + D_B + KV_W]
    for l in range(DEPTH):
        mod = (c_act @ w_mod[l] + b_mod[l]).reshape(B, 6, 1, D_MODEL)
        sh_m, sc_m, gt_m, sh_f, sc_f, gt_f = [mod[:, j] for j in range(6)]
        h = rms_norm(x, g_pre_mix[l]) * (1 + sc_m) + sh_m
        z = h @ w_in[l]
        u_a, v_a, q, k, v = jnp.split(z, splits, axis=-1)
        a_out, a_v = chunk_mlp_mixer(u_a, v_a, sgu_w[l], sgu_b[l])
        q = q.reshape(B, L, N_Q, HEAD_DIM)
        k = k.reshape(B, L, N_KV, HEAD_DIM)
        v = v.reshape(B, L, N_KV, HEAD_DIM)
        if sample:
            cache_k, cache_v, state_conv = cache
            b_out = window_attn_sample(q, k, v, cache_k[l], cache_v[l], attn_sink[l])
            ks.append(k)
            vs.append(v)
            sgus.append(a_v)
            st = state_conv[l]
        else:
            b_out = window_attn_prompt(q, k, v, attn_sink[l])
            ks.append(k[:, -WINDOW:])
            vs.append(v[:, -WINDOW:])
            st = jnp.zeros((B, CONV_W - 1, D_FF), x.dtype)
        mix = jnp.concatenate([rms_norm(a_out, g_out_a[l]), rms_norm(b_out, g_out_b[l])], axis=-1) @ w_out[l]
        x = x + gt_m * rms_norm(mix, g_post_mix[l])
        h = rms_norm(x, g_pre_ffn[l]) * (1 + sc_f) + sh_f
        f, new_conv = conv_ffn(h, w_up[l], conv_w[l], conv_b[l], w_down[l], st)
        convs.append(new_conv)
        x = x + gt_f * rms_norm(f, g_post_ffn[l])
    sgu = jnp.stack(sgus) if sample else None
    return x, jnp.stack(ks), jnp.stack(vs), jnp.stack(convs), sgu


def setup_inputs(seed: int = 0) -> dict:
    key = jax.random.key(seed)
    kk = jax.random.split(key, 32)
    f32 = jnp.float32
    nrm = lambda k, shape, s=1.0: (jax.random.normal(k, shape, f32) * s)
    gain = lambda k, shape: 1.0 + 0.05 * jax.random.normal(k, shape, f32)
    D = D_MODEL
    return {
        "x_prompt": nrm(kk[0], (BATCH, SEQ, D)),
        "x_sample": nrm(kk[1], (DEC_BATCH, DEC_SEQ, D)),
        "c_prompt": nrm(kk[2], (BATCH, D)),
        "c_sample": nrm(kk[3], (DEC_BATCH, D)),
        "cache_k": nrm(kk[4], (DEPTH, DEC_BATCH, WINDOW, N_KV, HEAD_DIM)),
        "cache_v": nrm(kk[5], (DEPTH, DEC_BATCH, WINDOW, N_KV, HEAD_DIM)),
        "state_conv": nrm(kk[6], (DEPTH, DEC_BATCH, CONV_W - 1, D_FF)),
        "w_mod": nrm(kk[7], (DEPTH, D, 6 * D), 0.5 * D ** -0.5),
        "b_mod": nrm(kk[8], (DEPTH, 6 * D), 0.1),
        "g_pre_mix": gain(kk[9], (DEPTH, D)),
        "g_post_mix": gain(kk[10], (DEPTH, D)),
        "w_in": nrm(kk[11], (DEPTH, D, IN_COLS), D ** -0.5),
        "sgu_w": nrm(kk[12], (DEPTH, A_GROUPS, A_BLOCK, A_BLOCK), 0.5 * A_BLOCK ** -0.5),
        "sgu_b": 1.0 + nrm(kk[13], (DEPTH, A_GROUPS, A_BLOCK), 0.05),
        "attn_sink": nrm(kk[14], (DEPTH, N_Q), 0.5),
        "g_out_a": gain(kk[15], (DEPTH, D_A)),
        "g_out_b": gain(kk[16], (DEPTH, D_B)),
        "w_out": nrm(kk[17], (DEPTH, D_MIX, D), D_MIX ** -0.5),
        "g_pre_ffn": gain(kk[18], (DEPTH, D)),
        "g_post_ffn": gain(kk[19], (DEPTH, D)),
        "w_up": nrm(kk[20], (DEPTH, D, 2 * D_FF), D ** -0.5),
        "conv_w": nrm(kk[21], (DEPTH, CONV_W, D_FF), CONV_W ** -0.5),
        "conv_b": nrm(kk[22], (DEPTH, D_FF), 0.02),
        "w_down": nrm(kk[23], (DEPTH, D_FF, D), D_FF ** -0.5),
    }


def reference(x_prompt, x_sample, c_prompt, c_sample, cache_k, cache_v, state_conv,
              w_mod, b_mod, g_pre_mix, g_post_mix, w_in, sgu_w, sgu_b, attn_sink,
              g_out_a, g_out_b, w_out, g_pre_ffn, g_post_ffn, w_up, conv_w, conv_b, w_down):
    params = (w_mod, b_mod, g_pre_mix, g_post_mix, w_in, sgu_w, sgu_b, attn_sink, g_out_a,
              g_out_b, w_out, g_pre_ffn, g_post_ffn, w_up, conv_w, conv_b, w_down)
    y_prompt, k_p, v_p, conv_p, _ = _trunk(x_prompt, c_prompt, params, None)
    y_sample, k_s, v_s, conv_s, sgu_s = _trunk(x_sample, c_sample, params,
                                               (cache_k, cache_v, state_conv))
    return (y_prompt, y_sample, k_p, v_p, k_s, v_s, conv_p, conv_s, sgu_s)
```

```python
import functools

import jax
import jax.numpy as jnp
from jax import lax
from jax.experimental import pallas as pl
from jax.experimental.pallas import tpu as pltpu

F32 = jnp.float32
BF16 = jnp.bfloat16

CHUNK = 64
EPS = 1e-6
NEG = -1e30
N_MOD = 6

MOD_TN = 512
NORM_TS = 256
PROJ_TM, PROJ_TN = 1024, 512
OUT_TM, OUT_TN = 1024, 1024
UP_TM, UP_TN = 1024, 256
DOWN_TM, DOWN_TN = 512, 512
SGU_ROWS = 256
ATTN_TQ = 256
MIB = 1024 * 1024


def _tile(full, pref):
    t = min(full, pref)
    assert full % t == 0, (full, pref)
    return t


def _params(vmem_mib, n_axes):
    return pltpu.CompilerParams(dimension_semantics=("arbitrary",) * n_axes,
                                vmem_limit_bytes=vmem_mib * MIB)


def _rms(x):
    return x * lax.rsqrt(jnp.mean(x * x, axis=-1, keepdims=True) + EPS)


def _mod_kernel(c_ref, w_ref, b_ref, o_ref):
    c = c_ref[...]
    act = jax.nn.silu(c).astype(BF16)
    o_ref[...] = jnp.dot(act, w_ref[...].astype(BF16), preferred_element_type=F32) + b_ref[...]


def _modulation(c, w_mod, b_mod):
    depth, d, n = w_mod.shape
    rows = c.shape[0]
    tn = _tile(n, MOD_TN)
    return pl.pallas_call(
        _mod_kernel,
        grid=(depth, n // tn),
        in_specs=[pl.BlockSpec((rows, d), lambda l, j: (0, 0)),
                  pl.BlockSpec((None, d, tn), lambda l, j: (l, 0, j)),
                  pl.BlockSpec((None, 1, tn), lambda l, j: (l, 0, j))],
        out_specs=pl.BlockSpec((None, rows, tn), lambda l, j: (l, 0, j)),
        out_shape=jax.ShapeDtypeStruct((depth, rows, n), F32),
        compiler_params=_params(40, 2),
        name="modulation",
    )(c, w_mod, b_mod.reshape(depth, 1, n))


def _prenorm_kernel(x_ref, g_ref, sc_ref, sh_ref, h_ref):
    y = _rms(x_ref[...]) * g_ref[...]
    h_ref[...] = (y * (1 + sc_ref[...]) + sh_ref[...]).astype(BF16)


def _mod_spec(boff, j, d):
    return pl.BlockSpec((None, None, 1, d), lambda b, t: (boff + b, j, 0, 0))


def _prenorm(x, g, mod, boff, j_sc, j_sh):
    bsz, s, d = x.shape
    ts = _tile(s, NORM_TS)
    return pl.pallas_call(
        _prenorm_kernel,
        grid=(bsz, s // ts),
        in_specs=[pl.BlockSpec((None, ts, d), lambda b, t: (b, t, 0)),
                  pl.BlockSpec((1, d), lambda b, t: (0, 0)),
                  _mod_spec(boff, j_sc, d), _mod_spec(boff, j_sh, d)],
        out_specs=pl.BlockSpec((None, ts, d), lambda b, t: (b, t, 0)),
        out_shape=jax.ShapeDtypeStruct((bsz, s, d), BF16),
        compiler_params=_params(32, 2),
        name="prenorm",
    )(x, g.reshape(1, d), mod, mod)


def _resid_kernel(x_ref, m_ref, gt_ref, gpost_ref, *rest, with_next):
    xn = x_ref[...] + gt_ref[...] * (_rms(m_ref[...]) * gpost_ref[...])
    if with_next:
        gpre_ref, sc_ref, sh_ref, xo_ref, h_ref = rest
        y = _rms(xn) * gpre_ref[...]
        h_ref[...] = (y * (1 + sc_ref[...]) + sh_ref[...]).astype(BF16)
    else:
        (xo_ref,) = rest
    xo_ref[...] = xn


def _resid(x, m, g_post, mod, boff, j_gt, nxt=None):
    bsz, s, d = x.shape
    ts = _tile(s, NORM_TS)
    row = pl.BlockSpec((None, ts, d), lambda b, t: (b, t, 0))
    vec = pl.BlockSpec((1, d), lambda b, t: (0, 0))
    in_specs = [row, row, _mod_spec(boff, j_gt, d), vec]
    args = [x, m.reshape(bsz, s, d), mod, g_post.reshape(1, d)]
    out_specs = [row]
    out_shape = [jax.ShapeDtypeStruct((bsz, s, d), F32)]
    if nxt is not None:
        g_pre, mod_next, j_sc, j_sh = nxt
        in_specs += [vec, _mod_spec(boff, j_sc, d), _mod_spec(boff, j_sh, d)]
        args += [g_pre.reshape(1, d), mod_next, mod_next]
        out_specs.append(row)
        out_shape.append(jax.ShapeDtypeStruct((bsz, s, d), BF16))
    out = pl.pallas_call(
        functools.partial(_resid_kernel, with_next=nxt is not None),
        grid=(bsz, s // ts),
        in_specs=in_specs, out_specs=out_specs, out_shape=out_shape,
        compiler_params=_params(48, 2),
        name="resid_norm",
    )(*args)
    return out if nxt is not None else (out[0], None)


def _mm_kernel(a_ref, w_ref, o_ref):
    o_ref[...] = jnp.dot(a_ref[...], w_ref[...], preferred_element_type=F32).astype(o_ref.dtype)


def _matmul(a, w, l, tm, tn, vmem_mib, name):
    m, k = a.shape
    n = w.shape[2]
    tm, tn = _tile(m, tm), _tile(n, tn)
    return pl.pallas_call(
        _mm_kernel,
        grid=(m // tm, n // tn),
        in_specs=[pl.BlockSpec((tm, k), lambda i, j: (i, 0)),
                  pl.BlockSpec((None, k, tn), lambda i, j: (l, 0, j))],
        out_specs=pl.BlockSpec((tm, tn), lambda i, j: (i, j)),
        out_shape=jax.ShapeDtypeStruct((m, n), F32),
        compiler_params=_params(vmem_mib, 2),
        name=name,
    )(a, w)


def _mm2_kernel(a1_ref, a2_ref, w1_ref, w2_ref, o_ref):
    o_ref[...] = (jnp.dot(a1_ref[...], w1_ref[...], preferred_element_type=F32)
                  + jnp.dot(a2_ref[...], w2_ref[...], preferred_element_type=F32))


def _out_proj(a1, a2, w, l):
    m, k = a1.shape
    assert a2.shape == (m, k) and w.shape[1] == 2 * k
    n = w.shape[2]
    tm, tn = _tile(m, OUT_TM), _tile(n, OUT_TN)
    return pl.pallas_call(
        _mm2_kernel,
        grid=(m // tm, n // tn),
        in_specs=[pl.BlockSpec((tm, k), lambda i, j: (i, 0)),
                  pl.BlockSpec((tm, k), lambda i, j: (i, 0)),
                  pl.BlockSpec((None, k, tn), lambda i, j: (l, 0, j)),
                  pl.BlockSpec((None, k, tn), lambda i, j: (l, 1, j))],
        out_specs=pl.BlockSpec((tm, tn), lambda i, j: (i, j)),
        out_shape=jax.ShapeDtypeStruct((m, n), F32),
        compiler_params=_params(48, 2),
        name="out_proj",
    )(a1, a2, w, w)


def _sgu_kernel(u_ref, v_ref, w_ref, bt_ref, g_ref, o_ref, *rest, n, groups, with_vn):
    if with_vn:
        vn_ref, acc_ref = rest
    else:
        (acc_ref,) = rest
    rows, d_a = u_ref.shape
    gd = d_a // groups
    qi = lax.broadcasted_iota(jnp.int32, (n, n), 0) // CHUNK
    kj = lax.broadcasted_iota(jnp.int32, (n, n), 1) // CHUNK
    causal = kj <= qi
    ssq = [jnp.zeros((n, 1), F32) for _ in range(rows // n)]
    for g in range(groups):
        wg = jnp.where(causal, w_ref[g], 0.0).astype(BF16)
        bias = bt_ref[:, g:g + 1]
        cols = slice(g * gd, (g + 1) * gd)
        for sb in range(rows // n):
            rs = slice(sb * n, (sb + 1) * n)
            vn = _rms(v_ref[rs, cols])
            if with_vn:
                vn_ref[rs, cols] = vn
            s = jnp.dot(wg, vn.astype(BF16), preferred_element_type=F32) + bias
            out = u_ref[rs, cols] * s
            acc_ref[rs, cols] = out
            ssq[sb] = ssq[sb] + jnp.sum(out * out, axis=-1, keepdims=True)
    for sb in range(rows // n):
        rs = slice(sb * n, (sb + 1) * n)
        inv = lax.rsqrt(ssq[sb] / d_a + EPS)
        o_ref[rs, :] = (acc_ref[rs, :] * inv * g_ref[...]).astype(BF16)


def _sgu(z, w_s, b_s, g_out, n, with_vn):
    m = z.shape[0]
    d_a = g_out.shape[0]
    groups = w_s.shape[0]
    rows = _tile(m, max(n, SGU_ROWS))
    out_specs = [pl.BlockSpec((rows, d_a), lambda i: (i, 0))]
    out_shape = [jax.ShapeDtypeStruct((m, d_a), BF16)]
    if with_vn:
        out_specs.append(pl.BlockSpec((rows, d_a), lambda i: (i, 0)))
        out_shape.append(jax.ShapeDtypeStruct((m, d_a), F32))
    return pl.pallas_call(
        functools.partial(_sgu_kernel, n=n, groups=groups, with_vn=with_vn),
        grid=(m // rows,),
        in_specs=[pl.BlockSpec((rows, d_a), lambda i: (i, 0)),
                  pl.BlockSpec((rows, d_a), lambda i: (i, 1)),
                  pl.BlockSpec((groups, n, n), lambda i: (0, 0, 0)),
                  pl.BlockSpec((n, groups), lambda i: (0, 0)),
                  pl.BlockSpec((1, d_a), lambda i: (0, 0))],
        out_specs=out_specs, out_shape=out_shape,
        scratch_shapes=[pltpu.VMEM((rows, d_a), F32)],
        compiler_params=_params(40, 1),
        name="sgu_mixer",
    )(z, z, w_s[:, :n, :n], b_s[:, :n].T, g_out.reshape(1, d_a))


def _attend(q_ref, kw, vw, mask, sink_ref, g_ref, o_ref, acc_ref, *, n_kv, hd):
    d_b = q_ref.shape[1]
    rep = d_b // (n_kv * hd)
    scale = hd ** -0.5
    for g in range(n_kv):
        kg = kw[:, g * hd:(g + 1) * hd].astype(BF16)
        vg = vw[:, g * hd:(g + 1) * hd].astype(BF16)
        for r in range(rep):
            h = g * rep + r
            cols = slice(h * hd, (h + 1) * hd)
            qh = q_ref[:, cols].astype(BF16)
            s = lax.dot_general(qh, kg, (((1,), (1,)), ((), ())), preferred_element_type=F32) * scale
            if mask is not None:
                s = jnp.where(mask, s, NEG)
            sk = sink_ref[h]
            mx = jnp.maximum(jnp.max(s, axis=-1, keepdims=True), sk)
            e = jnp.exp(s - mx)
            den = jnp.sum(e, axis=-1, keepdims=True) + jnp.exp(sk - mx)
            acc_ref[:, cols] = jnp.dot(e.astype(BF16), vg, preferred_element_type=F32) / den
    o_ref[...] = (_rms(acc_ref[...]) * g_ref[...]).astype(BF16)


def _attn_prompt_kernel(q_ref, k_ref, v_ref, sink_ref, g_ref, o_ref, acc_ref, *, window, n_kv, hd):
    tq = q_ref.shape[0]
    span = tq + window
    q0 = pl.program_id(1) * tq
    start = pl.multiple_of(jnp.maximum(q0 - window, 0), CHUNK)
    kw = k_ref[pl.ds(start, span), :]
    vw = v_ref[pl.ds(start, span), :]
    qc = (q0 + lax.broadcasted_iota(jnp.int32, (tq, span), 0)) // CHUNK
    kc = (start + lax.broadcasted_iota(jnp.int32, (tq, span), 1)) // CHUNK
    mask = jnp.logical_and(kc <= qc, kc >= qc - window // CHUNK)
    _attend(q_ref, kw, vw, mask, sink_ref, g_ref, o_ref, acc_ref, n_kv=n_kv, hd=hd)


def _attn_prompt(z3, sink, g_out, window, n_kv, hd):
    bsz, s, cols = z3.shape
    d_b = g_out.shape[0]
    kvw = n_kv * hd
    kblk = (cols - 2 * kvw) // kvw
    assert kblk * kvw == cols - 2 * kvw and (cols - 2 * kvw) == 3 * d_b
    tq = _tile(s, ATTN_TQ)
    assert tq % CHUNK == 0 and window % CHUNK == 0 and tq + window <= s
    return pl.pallas_call(
        functools.partial(_attn_prompt_kernel, window=window, n_kv=n_kv, hd=hd),
        grid=(bsz, s // tq),
        in_specs=[pl.BlockSpec((None, tq, d_b), lambda b, t: (b, t, 2)),
                  pl.BlockSpec((None, s, kvw), lambda b, t: (b, 0, kblk)),
                  pl.BlockSpec((None, s, kvw), lambda b, t: (b, 0, kblk + 1)),
                  pl.BlockSpec(memory_space=pltpu.SMEM),
                  pl.BlockSpec((1, d_b), lambda b, t: (0, 0))],
        out_specs=pl.BlockSpec((None, tq, d_b), lambda b, t: (b, t, 0)),
        out_shape=jax.ShapeDtypeStruct((bsz, s, d_b), BF16),
        scratch_shapes=[pltpu.VMEM((tq, d_b), F32)],
        compiler_params=_params(40, 2),
        name="attn_prompt",
    )(z3, z3, z3, sink, g_out.reshape(1, d_b))


def _attn_sample_kernel(q_ref, kn_ref, vn_ref, ck_ref, cv_ref, sink_ref, g_ref, o_ref, acc_ref, *, n_kv, hd):
    kw = jnp.concatenate([ck_ref[...], kn_ref[...]], axis=0)
    vw = jnp.concatenate([cv_ref[...], vn_ref[...]], axis=0)
    _attend(q_ref, kw, vw, None, sink_ref, g_ref, o_ref, acc_ref, n_kv=n_kv, hd=hd)


def _attn_sample(z3, ck, cv, sink, g_out, n_kv, hd):
    bsz, t, cols = z3.shape
    d_b = g_out.shape[0]
    kvw = n_kv * hd
    kblk = (cols - 2 * kvw) // kvw
    win = ck.shape[1]
    return pl.pallas_call(
        functools.partial(_attn_sample_kernel, n_kv=n_kv, hd=hd),
        grid=(bsz,),
        in_specs=[pl.BlockSpec((None, t, d_b), lambda b: (b, 0, 2)),
                  pl.BlockSpec((None, t, kvw), lambda b: (b, 0, kblk)),
                  pl.BlockSpec((None, t, kvw), lambda b: (b, 0, kblk + 1)),
                  pl.BlockSpec((None, win, kvw), lambda b: (b, 0, 0)),
                  pl.BlockSpec((None, win, kvw), lambda b: (b, 0, 0)),
                  pl.BlockSpec(memory_space=pltpu.SMEM),
                  pl.BlockSpec((1, d_b), lambda b: (0, 0))],
        out_specs=pl.BlockSpec((None, t, d_b), lambda b: (b, 0, 0)),
        out_shape=jax.ShapeDtypeStruct((bsz, t, d_b), BF16),
        scratch_shapes=[pltpu.VMEM((t, d_b), F32)],
        compiler_params=_params(32, 1),
        name="attn_sample",
    )(z3, z3, z3, ck.reshape(bsz, win, kvw), cv.reshape(bsz, win, kvw), sink, g_out.reshape(1, d_b))


def _conv_gate(a, b, prev, cw, cb):
    rows = a.shape[0]
    row = lax.broadcasted_iota(jnp.int32, a.shape, 0)
    a1 = jnp.where(row == 0, prev[1:2], pltpu.roll(a, 1, 0))
    a2 = jnp.where(row == 0, prev[0:1], jnp.where(row == 1, prev[1:2], pltpu.roll(a, 2, 0)))
    conv = cw[0:1] * a2 + cw[1:2] * a1 + cw[2:3] * a + cb
    return (jax.nn.silu(conv) * b).astype(BF16), a[rows - 2:rows]


def _up_kernel(h_ref, wa_ref, wb_ref, cw_ref, cb_ref, *rest, seq_tiles, pieces):
    if seq_tiles:
        act_ref, nc_ref, carry_ref = rest
    else:
        st_ref, act_ref, nc_ref = rest
    h = h_ref[...]
    a = jnp.dot(h, wa_ref[...], preferred_element_type=F32)
    b = jnp.dot(h, wb_ref[...], preferred_element_type=F32)
    cw = cw_ref[...]
    cb = cb_ref[...]
    if seq_tiles:
        i, j = pl.program_id(0), pl.program_id(1)

        @pl.when(i % seq_tiles == 0)
        def _():
            carry_ref[j] = jnp.zeros(carry_ref.shape[1:], F32)

        act, last = _conv_gate(a, b, carry_ref[j], cw, cb)
        act_ref[...] = act
        carry_ref[j] = last
        nc_ref[...] = last
    else:
        t = a.shape[0] // pieces
        for p in range(pieces):
            rs = slice(p * t, (p + 1) * t)
            act, last = _conv_gate(a[rs], b[rs], st_ref[p], cw, cb)
            act_ref[rs, :] = act
            nc_ref[p] = last


def _up_proj(h, w_up, conv_w, conv_b, l, bsz, state=None):
    m, d = h.shape
    d_ff = conv_w.shape[2]
    seq = m // bsz
    tn = _tile(d_ff, UP_TN)
    nj = d_ff // tn
    in_specs = [None,
                pl.BlockSpec((None, d, tn), lambda i, j: (l, 0, j)),
                pl.BlockSpec((None, d, tn), lambda i, j: (l, 0, nj + j)),
                pl.BlockSpec((None, conv_w.shape[1], tn), lambda i, j: (l, 0, j)),
                pl.BlockSpec((None, 1, tn), lambda i, j: (l, 0, j))]
    args = [h, w_up, w_up, conv_w, conv_b.reshape(conv_b.shape[0], 1, d_ff)]
    if state is None:
        tm = _tile(seq, UP_TM)
        seq_tiles = seq // tm
        nc_rows = m // tm
        nc_spec = pl.BlockSpec((None, 2, tn), lambda i, j: (i, 0, j))
        scratch = [pltpu.VMEM((nj, 2, tn), F32)]
        kern = functools.partial(_up_kernel, seq_tiles=seq_tiles, pieces=1)
    else:
        tm = m
        seq_tiles = 1
        nc_rows = bsz
        nc_spec = pl.BlockSpec((bsz, 2, tn), lambda i, j: (0, 0, j))
        in_specs.append(pl.BlockSpec((None, bsz, 2, tn), lambda i, j: (l, 0, 0, j)))
        args.append(state)
        scratch = []
        kern = functools.partial(_up_kernel, seq_tiles=0, pieces=bsz)
    in_specs[0] = pl.BlockSpec((tm, d), lambda i, j: (i, 0))
    act, last_rows = pl.pallas_call(
        kern,
        grid=(m // tm, nj),
        in_specs=in_specs,
        out_specs=[pl.BlockSpec((tm, tn), lambda i, j: (i, j)), nc_spec],
        out_shape=[jax.ShapeDtypeStruct((m, d_ff), BF16), jax.ShapeDtypeStruct((nc_rows, 2, d_ff), F32)],
        scratch_shapes=scratch,
        compiler_params=_params(48, 2),
        name="up_proj_conv_gate",
    )(*args)
    return act, last_rows[seq_tiles - 1::seq_tiles]


def _trunk(x, mods, boff, wts, cache):
    (g_pre_mix, g_post_mix, w_in, sgu_w, sgu_b, attn_sink, g_out_a, g_out_b, w_out,
     g_pre_ffn, g_post_ffn, w_up, conv_w, conv_b, w_down) = wts
    bsz, seq, d = x.shape
    depth = w_in.shape[0]
    m = bsz * seq
    d_a, d_b = g_out_a.shape[1], g_out_b.shape[1]
    n_kv, hd = cache[0].shape[3], cache[0].shape[4]
    window = cache[0].shape[2]
    kvw = n_kv * hd
    sample = cache[2] is not None
    n_blk = min(seq, sgu_w.shape[2])
    ks, vs, convs, sgus = [], [], [], []
    h = _prenorm(x, g_pre_mix[0], mods[0], boff, 1, 0)
    for l in range(depth):
        z = _matmul(h.reshape(m, d), w_in, l, PROJ_TM, PROJ_TN, 40, "in_proj")
        z3 = z.reshape(bsz, seq, z.shape[1])
        k = z3[:, :, 2 * d_a + d_b:2 * d_a + d_b + kvw].reshape(bsz, seq, n_kv, hd)
        v = z3[:, :, 2 * d_a + d_b + kvw:].reshape(bsz, seq, n_kv, hd)
        if sample:
            a_n, vn = _sgu(z, sgu_w[l], sgu_b[l], g_out_a[l], n_blk, True)
            b_n = _attn_sample(z3, cache[0][l], cache[1][l], attn_sink[l], g_out_b[l], n_kv, hd)
            ks.append(k)
            vs.append(v)
            sgus.append(vn.reshape(bsz, seq, d_a))
        else:
            (a_n,) = _sgu(z, sgu_w[l], sgu_b[l], g_out_a[l], n_blk, False)
            b_n = _attn_prompt(z3, attn_sink[l], g_out_b[l], window, n_kv, hd)
            ks.append(k[:, -window:])
            vs.append(v[:, -window:])
        mix = _out_proj(a_n, b_n.reshape(m, d_b), w_out, l)
        x, h = _resid(x, mix, g_post_mix[l], mods[l], boff, 2, (g_pre_ffn[l], mods[l], 4, 3))
        act, new_conv = _up_proj(h.reshape(m, d), w_up, conv_w, conv_b, l, bsz,
                                 cache[2] if sample else None)
        convs.append(new_conv)
        f = _matmul(act, w_down, l, DOWN_TM, DOWN_TN, 56, "down_proj")
        nxt = (g_pre_mix[l + 1], mods[l + 1], 1, 0) if l + 1 < depth else None
        x, h = _resid(x, f, g_post_ffn[l], mods[l], boff, 5, nxt)
    sgu = jnp.stack(sgus) if sample else None
    return x, jnp.stack(ks), jnp.stack(vs), jnp.stack(convs), sgu


def kernel(x_prompt, x_sample, c_prompt, c_sample, cache_k, cache_v, state_conv, w_mod, b_mod, g_pre_mix, g_post_mix, w_in, sgu_w, sgu_b, attn_sink, g_out_a, g_out_b, w_out, g_pre_ffn, g_post_ffn, w_up, conv_w, conv_b, w_down):
    depth, d = g_pre_mix.shape
    nb_p = c_prompt.shape[0]
    c_all = jnp.concatenate([c_prompt, c_sample], axis=0)
    mod = _modulation(c_all, w_mod, b_mod).reshape(depth, c_all.shape[0], N_MOD, 1, d)
    mods = [mod[l] for l in range(depth)]
    wts = (g_pre_mix, g_post_mix, w_in.astype(BF16), sgu_w, sgu_b, attn_sink, g_out_a, g_out_b,
           w_out.astype(BF16), g_pre_ffn, g_post_ffn, w_up.astype(BF16), conv_w, conv_b,
           w_down.astype(BF16))
    y_p, k_p, v_p, conv_p, _ = _trunk(x_prompt, mods, 0, wts, (cache_k, cache_v, None))
    y_s, k_s, v_s, conv_s, sgu_s = _trunk(x_sample, mods, nb_p, wts, (cache_k, cache_v, state_conv))
    return (y_p, y_s, k_p, v_p, k_s, v_s, conv_p, conv_s, sgu_s)
```

```python
import functools

import jax
import jax.numpy as jnp
from jax import lax
from jax.experimental import pallas as pl
from jax.experimental.pallas import tpu as pltpu

F32 = jnp.float32
BF16 = jnp.bfloat16

CHUNK = 64
EPS = 1e-6
NEG = -1e30
N_MOD = 6
LOG2E = 1.4426950408889634
LANES = 128

MOD_TN = 512
NORM_TS = 256
PROJ_TM, PROJ_TN = 1024, 512
OUT_TM, OUT_TN = 1024, 1024
UP_TM, UP_TN = 2048, 256
UP_RC = 256
DOWN_TM, DOWN_TN = 512, 512
SGU_ROWS = 256
ATTN_TQ = 128
MIB = 1024 * 1024


def _tile(full, pref):
    t = min(full, pref)
    assert full % t == 0, (full, pref)
    return t


def _params(vmem_mib, n_axes):
    return pltpu.CompilerParams(dimension_semantics=("arbitrary",) * n_axes,
                                vmem_limit_bytes=vmem_mib * MIB)


def _rms(x):
    return x * lax.rsqrt(jnp.mean(x * x, axis=-1, keepdims=True) + EPS)


def _mod_kernel(c_ref, w_ref, b_ref, o_ref):
    c = c_ref[...]
    act = jax.nn.silu(c).astype(BF16)
    o_ref[...] = jnp.dot(act, w_ref[...].astype(BF16), preferred_element_type=F32) + b_ref[...]


def _modulation(c, w_mod, b_mod):
    depth, d, n = w_mod.shape
    rows = c.shape[0]
    tn = _tile(n, MOD_TN)
    return pl.pallas_call(
        _mod_kernel,
        grid=(depth, n // tn),
        in_specs=[pl.BlockSpec((rows, d), lambda l, j: (0, 0)),
                  pl.BlockSpec((None, d, tn), lambda l, j: (l, 0, j)),
                  pl.BlockSpec((None, 1, tn), lambda l, j: (l, 0, j))],
        out_specs=pl.BlockSpec((None, rows, tn), lambda l, j: (l, 0, j)),
        out_shape=jax.ShapeDtypeStruct((depth, rows, n), F32),
        compiler_params=_params(40, 2),
        name="modulation",
    )(c, w_mod, b_mod.reshape(depth, 1, n))


def _prenorm_kernel(x_ref, g_ref, sc_ref, sh_ref, h_ref):
    y = _rms(x_ref[...]) * g_ref[...]
    h_ref[...] = (y * (1 + sc_ref[...]) + sh_ref[...]).astype(BF16)


def _mod_spec(boff, j, d):
    return pl.BlockSpec((None, None, 1, d), lambda b, t: (boff + b, j, 0, 0))


def _prenorm(x, g, mod, boff, j_sc, j_sh):
    bsz, s, d = x.shape
    ts = _tile(s, NORM_TS)
    return pl.pallas_call(
        _prenorm_kernel,
        grid=(bsz, s // ts),
        in_specs=[pl.BlockSpec((None, ts, d), lambda b, t: (b, t, 0)),
                  pl.BlockSpec((1, d), lambda b, t: (0, 0)),
                  _mod_spec(boff, j_sc, d), _mod_spec(boff, j_sh, d)],
        out_specs=pl.BlockSpec((None, ts, d), lambda b, t: (b, t, 0)),
        out_shape=jax.ShapeDtypeStruct((bsz, s, d), BF16),
        compiler_params=_params(32, 2),
        name="prenorm",
    )(x, g.reshape(1, d), mod, mod)


def _resid_kernel(x_ref, m_ref, gt_ref, gpost_ref, *rest, with_next):
    xn = x_ref[...] + gt_ref[...] * (_rms(m_ref[...]) * gpost_ref[...])
    if with_next:
        gpre_ref, sc_ref, sh_ref, xo_ref, h_ref = rest
        y = _rms(xn) * gpre_ref[...]
        h_ref[...] = (y * (1 + sc_ref[...]) + sh_ref[...]).astype(BF16)
    else:
        (xo_ref,) = rest
    xo_ref[...] = xn


def _resid(x, m, g_post, mod, boff, j_gt, nxt=None):
    bsz, s, d = x.shape
    ts = _tile(s, NORM_TS)
    row = pl.BlockSpec((None, ts, d), lambda b, t: (b, t, 0))
    vec = pl.BlockSpec((1, d), lambda b, t: (0, 0))
    in_specs = [row, row, _mod_spec(boff, j_gt, d), vec]
    args = [x, m.reshape(bsz, s, d), mod, g_post.reshape(1, d)]
    out_specs = [row]
    out_shape = [jax.ShapeDtypeStruct((bsz, s, d), F32)]
    if nxt is not None:
        g_pre, mod_next, j_sc, j_sh = nxt
        in_specs += [vec, _mod_spec(boff, j_sc, d), _mod_spec(boff, j_sh, d)]
        args += [g_pre.reshape(1, d), mod_next, mod_next]
        out_specs.append(row)
        out_shape.append(jax.ShapeDtypeStruct((bsz, s, d), BF16))
    out = pl.pallas_call(
        functools.partial(_resid_kernel, with_next=nxt is not None),
        grid=(bsz, s // ts),
        in_specs=in_specs, out_specs=out_specs, out_shape=out_shape,
        compiler_params=_params(48, 2),
        name="resid_norm",
    )(*args)
    return out if nxt is not None else (out[0], None)


def _mm_kernel(a_ref, w_ref, o_ref):
    o_ref[...] = jnp.dot(a_ref[...], w_ref[...], preferred_element_type=F32).astype(o_ref.dtype)


def _matmul(a, w, l, tm, tn, vmem_mib, name):
    m, k = a.shape
    n = w.shape[2]
    tm, tn = _tile(m, tm), _tile(n, tn)
    return pl.pallas_call(
        _mm_kernel,
        grid=(m // tm, n // tn),
        in_specs=[pl.BlockSpec((tm, k), lambda i, j: (i, 0)),
                  pl.BlockSpec((None, k, tn), lambda i, j: (l, 0, j))],
        out_specs=pl.BlockSpec((tm, tn), lambda i, j: (i, j)),
        out_shape=jax.ShapeDtypeStruct((m, n), F32),
        compiler_params=_params(vmem_mib, 2),
        name=name,
    )(a, w)


def _mm2_kernel(a1_ref, a2_ref, w1_ref, w2_ref, o_ref):
    o_ref[...] = (jnp.dot(a1_ref[...], w1_ref[...], preferred_element_type=F32)
                  + jnp.dot(a2_ref[...], w2_ref[...], preferred_element_type=F32))


def _out_proj(a1, a2, w, l):
    m, k = a1.shape
    assert a2.shape == (m, k) and w.shape[1] == 2 * k
    n = w.shape[2]
    tm, tn = _tile(m, OUT_TM), _tile(n, OUT_TN)
    return pl.pallas_call(
        _mm2_kernel,
        grid=(m // tm, n // tn),
        in_specs=[pl.BlockSpec((tm, k), lambda i, j: (i, 0)),
                  pl.BlockSpec((tm, k), lambda i, j: (i, 0)),
                  pl.BlockSpec((None, k, tn), lambda i, j: (l, 0, j)),
                  pl.BlockSpec((None, k, tn), lambda i, j: (l, 1, j))],
        out_specs=pl.BlockSpec((tm, tn), lambda i, j: (i, j)),
        out_shape=jax.ShapeDtypeStruct((m, n), F32),
        compiler_params=_params(48, 2),
        name="out_proj",
    )(a1, a2, w, w)


def _sgu_kernel(u_ref, v_ref, w_ref, bt_ref, g_ref, o_ref, *rest, n, groups, with_vn):
    if with_vn:
        vn_ref, acc_ref = rest
    else:
        (acc_ref,) = rest
    rows, d_a = u_ref.shape
    gd = d_a // groups
    qi = lax.broadcasted_iota(jnp.int32, (n, n), 0) // CHUNK
    kj = lax.broadcasted_iota(jnp.int32, (n, n), 1) // CHUNK
    causal = kj <= qi
    ssq = [jnp.zeros((n, 1), F32) for _ in range(rows // n)]
    for g in range(groups):
        wg = jnp.where(causal, w_ref[g], 0.0).astype(BF16)
        bias = bt_ref[:, g:g + 1]
        cols = slice(g * gd, (g + 1) * gd)
        for sb in range(rows // n):
            rs = slice(sb * n, (sb + 1) * n)
            vn = _rms(v_ref[rs, cols])
            if with_vn:
                vn_ref[rs, cols] = vn
            s = jnp.dot(wg, vn.astype(BF16), preferred_element_type=F32) + bias
            out = u_ref[rs, cols] * s
            acc_ref[rs, cols] = out
            ssq[sb] = ssq[sb] + jnp.sum(out * out, axis=-1, keepdims=True)
    for sb in range(rows // n):
        rs = slice(sb * n, (sb + 1) * n)
        inv = lax.rsqrt(ssq[sb] / d_a + EPS)
        o_ref[rs, :] = (acc_ref[rs, :] * inv * g_ref[...]).astype(BF16)


def _sgu(z, w_s, b_s, g_out, n, with_vn):
    m = z.shape[0]
    d_a = g_out.shape[0]
    groups = w_s.shape[0]
    rows = _tile(m, max(n, SGU_ROWS))
    out_specs = [pl.BlockSpec((rows, d_a), lambda i: (i, 0))]
    out_shape = [jax.ShapeDtypeStruct((m, d_a), BF16)]
    if with_vn:
        out_specs.append(pl.BlockSpec((rows, d_a), lambda i: (i, 0)))
        out_shape.append(jax.ShapeDtypeStruct((m, d_a), F32))
    return pl.pallas_call(
        functools.partial(_sgu_kernel, n=n, groups=groups, with_vn=with_vn),
        grid=(m // rows,),
        in_specs=[pl.BlockSpec((rows, d_a), lambda i: (i, 0)),
                  pl.BlockSpec((rows, d_a), lambda i: (i, 1)),
                  pl.BlockSpec((groups, n, n), lambda i: (0, 0, 0)),
                  pl.BlockSpec((n, groups), lambda i: (0, 0)),
                  pl.BlockSpec((1, d_a), lambda i: (0, 0))],
        out_specs=out_specs, out_shape=out_shape,
        scratch_shapes=[pltpu.VMEM((rows, d_a), F32)],
        compiler_params=_params(40, 1),
        name="sgu_mixer",
    )(z, z, w_s[:, :n, :n], b_s[:, :n].T, g_out.reshape(1, d_a))


def _attend(q_ref, kw, vw, bias, sink_ref, g_ref, o_ref, acc_ref, *, n_kv, hd):
    tq, d_b = q_ref.shape
    span = kw.shape[0]
    rep = d_b // (n_kv * hd)
    assert 2 * hd == LANES and rep % 2 == 0
    qscale = hd ** -0.5 * LOG2E
    zeros = jnp.zeros((span, hd), BF16)
    ones = jnp.ones((span, hd), BF16)
    first = lax.broadcasted_iota(jnp.int32, (tq, LANES), 1) < hd
    nt = (((1,), (1,)), ((), ()))
    for g in range(n_kv):
        kg = kw[:, g * hd:(g + 1) * hd].astype(BF16)
        vg = vw[:, g * hd:(g + 1) * hd].astype(BF16)
        keys = (jnp.concatenate([kg, zeros], axis=1), jnp.concatenate([zeros, kg], axis=1))
        vals = (jnp.concatenate([vg, zeros, ones, zeros], axis=1),
                jnp.concatenate([zeros, vg, zeros, ones], axis=1))
        for p in range(rep // 2):
            h0 = g * rep + 2 * p
            cols = slice(h0 * hd, (h0 + 2) * hd)
            qp = (q_ref[:, cols] * qscale).astype(BF16)
            out, sink_e = None, []
            for i in range(2):
                s = lax.dot_general(qp, keys[i], nt, preferred_element_type=F32)
                if bias is not None:
                    s = s + bias
                sk = sink_ref[h0 + i] * LOG2E
                mx = jnp.maximum(jnp.max(s, axis=-1, keepdims=True), sk)
                e = jnp.exp2(s - mx).astype(BF16)
                sink_e.append(jnp.exp2(sk - mx))
                o = jnp.dot(e, vals[i], preferred_element_type=F32)
                out = o if out is None else out + o
            den = out[:, LANES:] + jnp.where(first, sink_e[0], sink_e[1])
            acc_ref[:, cols] = out[:, :LANES] / den
    o_ref[...] = (_rms(acc_ref[...]) * g_ref[...]).astype(BF16)


def _attn_prompt_kernel(q_ref, k_ref, v_ref, sink_ref, g_ref, o_ref, acc_ref, *, window, n_kv, hd):
    tq = q_ref.shape[0]
    span = tq + window
    q0 = pl.program_id(1) * tq
    start = pl.multiple_of(jnp.maximum(q0 - window, 0), CHUNK)
    kw = k_ref[pl.ds(start, span), :]
    vw = v_ref[pl.ds(start, span), :]
    qc = (q0 + lax.broadcasted_iota(jnp.int32, (tq, span), 0)) // CHUNK
    kc = (start + lax.broadcasted_iota(jnp.int32, (tq, span), 1)) // CHUNK
    bias = jnp.where(kc <= qc, jnp.where(kc >= qc - window // CHUNK, 0.0, NEG), NEG)
    _attend(q_ref, kw, vw, bias, sink_ref, g_ref, o_ref, acc_ref, n_kv=n_kv, hd=hd)


def _attn_prompt(z3, sink, g_out, window, n_kv, hd):
    bsz, s, cols = z3.shape
    d_b = g_out.shape[0]
    kvw = n_kv * hd
    kblk = (cols - 2 * kvw) // kvw
    assert kblk * kvw == cols - 2 * kvw and (cols - 2 * kvw) == 3 * d_b
    tq = _tile(s, ATTN_TQ)
    assert tq % CHUNK == 0 and window % CHUNK == 0 and tq + window <= s
    return pl.pallas_call(
        functools.partial(_attn_prompt_kernel, window=window, n_kv=n_kv, hd=hd),
        grid=(bsz, s // tq),
        in_specs=[pl.BlockSpec((None, tq, d_b), lambda b, t: (b, t, 2)),
                  pl.BlockSpec((None, s, kvw), lambda b, t: (b, 0, kblk)),
                  pl.BlockSpec((None, s, kvw), lambda b, t: (b, 0, kblk + 1)),
                  pl.BlockSpec(memory_space=pltpu.SMEM),
                  pl.BlockSpec((1, d_b), lambda b, t: (0, 0))],
        out_specs=pl.BlockSpec((None, tq, d_b), lambda b, t: (b, t, 0)),
        out_shape=jax.ShapeDtypeStruct((bsz, s, d_b), BF16),
        scratch_shapes=[pltpu.VMEM((tq, d_b), F32)],
        compiler_params=_params(40, 2),
        name="attn_prompt",
    )(z3, z3, z3, sink, g_out.reshape(1, d_b))


def _attn_sample_kernel(q_ref, kn_ref, vn_ref, ck_ref, cv_ref, sink_ref, g_ref, o_ref, acc_ref, *, n_kv, hd):
    kw = jnp.concatenate([ck_ref[...], kn_ref[...]], axis=0)
    vw = jnp.concatenate([cv_ref[...], vn_ref[...]], axis=0)
    _attend(q_ref, kw, vw, None, sink_ref, g_ref, o_ref, acc_ref, n_kv=n_kv, hd=hd)


def _attn_sample(z3, ck, cv, sink, g_out, n_kv, hd):
    bsz, t, cols = z3.shape
    d_b = g_out.shape[0]
    kvw = n_kv * hd
    kblk = (cols - 2 * kvw) // kvw
    win = ck.shape[1]
    return pl.pallas_call(
        functools.partial(_attn_sample_kernel, n_kv=n_kv, hd=hd),
        grid=(bsz,),
        in_specs=[pl.BlockSpec((None, t, d_b), lambda b: (b, 0, 2)),
                  pl.BlockSpec((None, t, kvw), lambda b: (b, 0, kblk)),
                  pl.BlockSpec((None, t, kvw), lambda b: (b, 0, kblk + 1)),
                  pl.BlockSpec((None, win, kvw), lambda b: (b, 0, 0)),
                  pl.BlockSpec((None, win, kvw), lambda b: (b, 0, 0)),
                  pl.BlockSpec(memory_space=pltpu.SMEM),
                  pl.BlockSpec((1, d_b), lambda b: (0, 0))],
        out_specs=pl.BlockSpec((None, t, d_b), lambda b: (b, 0, 0)),
        out_shape=jax.ShapeDtypeStruct((bsz, t, d_b), BF16),
        scratch_shapes=[pltpu.VMEM((t, d_b), F32)],
        compiler_params=_params(32, 1),
        name="attn_sample",
    )(z3, z3, z3, ck.reshape(bsz, win, kvw), cv.reshape(bsz, win, kvw), sink, g_out.reshape(1, d_b))


def _conv_gate(a, b, prev, cw, cb):
    rows = a.shape[0]
    row = lax.broadcasted_iota(jnp.int32, a.shape, 0)
    a1 = jnp.where(row == 0, prev[1:2], pltpu.roll(a, 1, 0))
    a2 = jnp.where(row == 0, prev[0:1], jnp.where(row == 1, prev[1:2], pltpu.roll(a, 2, 0)))
    conv = cw[0:1] * a2 + cw[1:2] * a1 + cw[2:3] * a + cb
    return (jax.nn.silu(conv) * b).astype(BF16), a[rows - 2:rows]


def _up_kernel(h_ref, wa_ref, wb_ref, cw_ref, cb_ref, *rest, seq_tiles, pieces):
    if seq_tiles:
        act_ref, nc_ref, carry_ref = rest
    else:
        st_ref, act_ref, nc_ref = rest
    cw = cw_ref[...]
    cb = cb_ref[...]
    tm = h_ref.shape[0]
    if seq_tiles:
        i, j = pl.program_id(0), pl.program_id(1)

        @pl.when(i % seq_tiles == 0)
        def _():
            carry_ref[j] = jnp.zeros(carry_ref.shape[1:], F32)

        last = carry_ref[j]
        rc = _tile(tm, UP_RC)
        for c in range(tm // rc):
            rs = slice(c * rc, (c + 1) * rc)
            h = h_ref[rs, :]
            a = jnp.dot(h, wa_ref[...], preferred_element_type=F32)
            b = jnp.dot(h, wb_ref[...], preferred_element_type=F32)
            act_ref[rs, :], last = _conv_gate(a, b, last, cw, cb)
        carry_ref[j] = last
        nc_ref[...] = last
    else:
        h = h_ref[...]
        a = jnp.dot(h, wa_ref[...], preferred_element_type=F32)
        b = jnp.dot(h, wb_ref[...], preferred_element_type=F32)
        t = tm // pieces
        for p in range(pieces):
            rs = slice(p * t, (p + 1) * t)
            act_ref[rs, :], nc_ref[p] = _conv_gate(a[rs], b[rs], st_ref[p], cw, cb)


def _up_proj(h, w_up, conv_w, conv_b, l, bsz, state=None):
    m, d = h.shape
    d_ff = conv_w.shape[2]
    seq = m // bsz
    tn = _tile(d_ff, UP_TN)
    nj = d_ff // tn
    in_specs = [None,
                pl.BlockSpec((None, d, tn), lambda i, j: (l, 0, j)),
                pl.BlockSpec((None, d, tn), lambda i, j: (l, 0, nj + j)),
                pl.BlockSpec((None, conv_w.shape[1], tn), lambda i, j: (l, 0, j)),
                pl.BlockSpec((None, 1, tn), lambda i, j: (l, 0, j))]
    args = [h, w_up, w_up, conv_w, conv_b.reshape(conv_b.shape[0], 1, d_ff)]
    if state is None:
        tm = _tile(seq, UP_TM)
        seq_tiles = seq // tm
        nc_rows = m // tm
        nc_spec = pl.BlockSpec((None, 2, tn), lambda i, j: (i, 0, j))
        scratch = [pltpu.VMEM((nj, 2, tn), F32)]
        kern = functools.partial(_up_kernel, seq_tiles=seq_tiles, pieces=1)
    else:
        tm = m
        seq_tiles = 1
        nc_rows = bsz
        nc_spec = pl.BlockSpec((bsz, 2, tn), lambda i, j: (0, 0, j))
        in_specs.append(pl.BlockSpec((None, bsz, 2, tn), lambda i, j: (l, 0, 0, j)))
        args.append(state)
        scratch = []
        kern = functools.partial(_up_kernel, seq_tiles=0, pieces=bsz)
    in_specs[0] = pl.BlockSpec((tm, d), lambda i, j: (i, 0))
    act, last_rows = pl.pallas_call(
        kern,
        grid=(m // tm, nj),
        in_specs=in_specs,
        out_specs=[pl.BlockSpec((tm, tn), lambda i, j: (i, j)), nc_spec],
        out_shape=[jax.ShapeDtypeStruct((m, d_ff), BF16), jax.ShapeDtypeStruct((nc_rows, 2, d_ff), F32)],
        scratch_shapes=scratch,
        compiler_params=_params(56, 2),
        name="up_proj_conv_gate",
    )(*args)
    return act, last_rows[seq_tiles - 1::seq_tiles]


def _trunk(x, mods, boff, wts, cache):
    (g_pre_mix, g_post_mix, w_in, sgu_w, sgu_b, attn_sink, g_out_a, g_out_b, w_out,
     g_pre_ffn, g_post_ffn, w_up, conv_w, conv_b, w_down) = wts
    bsz, seq, d = x.shape
    depth = w_in.shape[0]
    m = bsz * seq
    d_a, d_b = g_out_a.shape[1], g_out_b.shape[1]
    n_kv, hd = cache[0].shape[3], cache[0].shape[4]
    window = cache[0].shape[2]
    kvw = n_kv * hd
    sample = cache[2] is not None
    n_blk = min(seq, sgu_w.shape[2])
    ks, vs, convs, sgus = [], [], [], []
    h = _prenorm(x, g_pre_mix[0], mods[0], boff, 1, 0)
    for l in range(depth):
        z = _matmul(h.reshape(m, d), w_in, l, PROJ_TM, PROJ_TN, 40, "in_proj")
        z3 = z.reshape(bsz, seq, z.shape[1])
        k = z3[:, :, 2 * d_a + d_b:2 * d_a + d_b + kvw].reshape(bsz, seq, n_kv, hd)
        v = z3[:, :, 2 * d_a + d_b + kvw:].reshape(bsz, seq, n_kv, hd)
        if sample:
            a_n, vn = _sgu(z, sgu_w[l], sgu_b[l], g_out_a[l], n_blk, True)
            b_n = _attn_sample(z3, cache[0][l], cache[1][l], attn_sink[l], g_out_b[l], n_kv, hd)
            ks.append(k)
            vs.append(v)
            sgus.append(vn.reshape(bsz, seq, d_a))
        else:
            (a_n,) = _sgu(z, sgu_w[l], sgu_b[l], g_out_a[l], n_blk, False)
            b_n = _attn_prompt(z3, attn_sink[l], g_out_b[l], window, n_kv, hd)
            ks.append(k[:, -window:])
            vs.append(v[:, -window:])
        mix = _out_proj(a_n, b_n.reshape(m, d_b), w_out, l)
        x, h = _resid(x, mix, g_post_mix[l], mods[l], boff, 2, (g_pre_ffn[l], mods[l], 4, 3))
        act, new_conv = _up_proj(h.reshape(m, d), w_up, conv_w, conv_b, l, bsz,
                                 cache[2] if sample else None)
        convs.append(new_conv)
        f = _matmul(act, w_down, l, DOWN_TM, DOWN_TN, 56, "down_proj")
        nxt = (g_pre_mix[l + 1], mods[l + 1], 1, 0) if l + 1 < depth else None
        x, h = _resid(x, f, g_post_ffn[l], mods[l], boff, 5, nxt)
    sgu = jnp.stack(sgus) if sample else None
    return x, jnp.stack(ks), jnp.stack(vs), jnp.stack(convs), sgu


def kernel(x_prompt, x_sample, c_prompt, c_sample, cache_k, cache_v, state_conv, w_mod, b_mod, g_pre_mix, g_post_mix, w_in, sgu_w, sgu_b, attn_sink, g_out_a, g_out_b, w_out, g_pre_ffn, g_post_ffn, w_up, conv_w, conv_b, w_down):
    depth, d = g_pre_mix.shape
    nb_p = c_prompt.shape[0]
    c_all = jnp.concatenate([c_prompt, c_sample], axis=0)
    mod = _modulation(c_all, w_mod, b_mod).reshape(depth, c_all.shape[0], N_MOD, 1, d)
    mods = [mod[l] for l in range(depth)]
    wts = (g_pre_mix, g_post_mix, w_in.astype(BF16), sgu_w, sgu_b, attn_sink, g_out_a, g_out_b,
           w_out.astype(BF16), g_pre_ffn, g_post_ffn, w_up.astype(BF16), conv_w, conv_b,
           w_down.astype(BF16))
    y_p, k_p, v_p, conv_p, _ = _trunk(x_prompt, mods, 0, wts, (cache_k, cache_v, None))
    y_s, k_s, v_s, conv_s, sgu_s = _trunk(x_sample, mods, nb_p, wts, (cache_k, cache_v, state_conv))
    return (y_p, y_s, k_p, v_p, k_s, v_s, conv_p, conv_s, sgu_s)
```

```python
import functools

import jax
import jax.numpy as jnp
from jax import lax
from jax.experimental import pallas as pl
from jax.experimental.pallas import tpu as pltpu

F32 = jnp.float32
BF16 = jnp.bfloat16

CHUNK = 64
EPS = 1e-6
NEG = -1e30
N_MOD = 6
LOG2E = 1.4426950408889634
LANES = 128
BF16_ROWS = 16

MOD_TN = 512
NORM_TS = 256
PROJ_TM, PROJ_TN = 1024, 512
OUT_TM, OUT_TN = 1024, 512
UP_TN = 256
UP_RC = 256
DOWN_TM, DOWN_TN = 512, 512
SGU_ROWS = 256
ATTN_TQ = 128
MIB = 1024 * 1024


def _tile(full, pref):
    t = min(full, pref)
    assert full % t == 0, (full, pref)
    return t


def _params(vmem_mib, n_axes):
    return pltpu.CompilerParams(dimension_semantics=("arbitrary",) * n_axes,
                                vmem_limit_bytes=vmem_mib * MIB)


def _rms(x):
    return x * lax.rsqrt(jnp.mean(x * x, axis=-1, keepdims=True) + EPS)


def _with_casts(body, n_in, n_out, n_cast):
    if not n_cast:
        return body

    def kernel(*refs):
        ins = refs[:n_in]
        cast_in = refs[n_in:n_in + n_cast]
        outs = refs[n_in + n_cast:n_in + n_cast + n_out]
        cast_out = refs[n_in + n_cast + n_out:n_in + 2 * n_cast + n_out]
        for src, dst in zip(cast_in, cast_out):
            dst[...] = src[...].astype(BF16)
        body(*ins, *outs, *refs[n_in + 2 * n_cast + n_out:])

    return kernel


def _cast_specs(casts, n_steps, step_of):
    specs, shapes = [], []
    for w in casts:
        rows, cols = w.shape
        slab = rows // n_steps
        assert slab * n_steps == rows and slab % BF16_ROWS == 0, (w.shape, n_steps)
        specs.append(pl.BlockSpec((slab, cols), lambda *g: (jnp.minimum(step_of(*g), n_steps - 1), 0)))
        shapes.append(jax.ShapeDtypeStruct((rows, cols), BF16))
    return specs, shapes


def _mod_kernel(c_ref, w_ref, b_ref, o_ref):
    act = jax.nn.silu(c_ref[...]).astype(BF16)
    o_ref[...] = jnp.dot(act, w_ref[...].astype(BF16), preferred_element_type=F32) + b_ref[...]


def _modulation(c, w_mod, b_mod):
    depth, d, n = w_mod.shape
    rows = c.shape[0]
    tn = _tile(n, MOD_TN)
    return pl.pallas_call(
        _mod_kernel,
        grid=(depth, n // tn),
        in_specs=[pl.BlockSpec((rows, d), lambda l, j: (0, 0)),
                  pl.BlockSpec((None, d, tn), lambda l, j: (l, 0, j)),
                  pl.BlockSpec((None, 1, tn), lambda l, j: (l, 0, j))],
        out_specs=pl.BlockSpec((None, rows, tn), lambda l, j: (l, 0, j)),
        out_shape=jax.ShapeDtypeStruct((depth, rows, n), F32),
        compiler_params=_params(40, 2),
        name="modulation",
    )(c, w_mod, b_mod.reshape(depth, 1, n))


def _prenorm_kernel(x_ref, g_ref, sc_ref, sh_ref, h_ref):
    y = _rms(x_ref[...]) * g_ref[...]
    h_ref[...] = (y * (1 + sc_ref[...]) + sh_ref[...]).astype(BF16)


def _mod_spec(boff, j, d):
    return pl.BlockSpec((None, None, 1, d), lambda b, t: (boff + b, j, 0, 0))


def _prenorm(x, g, mod, boff, j_sc, j_sh):
    bsz, s, d = x.shape
    ts = _tile(s, NORM_TS)
    return pl.pallas_call(
        _prenorm_kernel,
        grid=(bsz, s // ts),
        in_specs=[pl.BlockSpec((None, ts, d), lambda b, t: (b, t, 0)),
                  pl.BlockSpec((1, d), lambda b, t: (0, 0)),
                  _mod_spec(boff, j_sc, d), _mod_spec(boff, j_sh, d)],
        out_specs=pl.BlockSpec((None, ts, d), lambda b, t: (b, t, 0)),
        out_shape=jax.ShapeDtypeStruct((bsz, s, d), BF16),
        compiler_params=_params(32, 2),
        name="prenorm",
    )(x, g.reshape(1, d), mod, mod)


def _resid_kernel(x_ref, m_ref, gt_ref, gpost_ref, *rest, with_next):
    xn = x_ref[...] + gt_ref[...] * (_rms(m_ref[...].astype(F32)) * gpost_ref[...])
    if with_next:
        gpre_ref, sc_ref, sh_ref, xo_ref, h_ref = rest
        y = _rms(xn) * gpre_ref[...]
        h_ref[...] = (y * (1 + sc_ref[...]) + sh_ref[...]).astype(BF16)
    else:
        (xo_ref,) = rest
    xo_ref[...] = xn


def _resid(x, m, g_post, mod, boff, j_gt, nxt=None):
    bsz, s, d = x.shape
    ts = _tile(s, NORM_TS)
    row = pl.BlockSpec((None, ts, d), lambda b, t: (b, t, 0))
    vec = pl.BlockSpec((1, d), lambda b, t: (0, 0))
    in_specs = [row, row, _mod_spec(boff, j_gt, d), vec]
    args = [x, m.reshape(bsz, s, d), mod, g_post.reshape(1, d)]
    out_specs = [row]
    out_shape = [jax.ShapeDtypeStruct((bsz, s, d), F32)]
    if nxt is not None:
        g_pre, mod_next, j_sc, j_sh = nxt
        in_specs += [vec, _mod_spec(boff, j_sc, d), _mod_spec(boff, j_sh, d)]
        args += [g_pre.reshape(1, d), mod_next, mod_next]
        out_specs.append(row)
        out_shape.append(jax.ShapeDtypeStruct((bsz, s, d), BF16))
    out = pl.pallas_call(
        functools.partial(_resid_kernel, with_next=nxt is not None),
        grid=(bsz, s // ts),
        in_specs=in_specs, out_specs=out_specs, out_shape=out_shape,
        compiler_params=_params(48, 2),
        name="resid_norm",
    )(*args)
    return out if nxt is not None else (out[0], None)


def _in_proj_kernel(a_ref, w_ref, zm_ref, zkv_ref):
    r = jnp.dot(a_ref[...], w_ref[...], preferred_element_type=F32)
    zm_ref[...] = r.astype(BF16)

    @pl.when(pl.program_id(1) == 0)
    def _():
        zkv_ref[...] = r


def _in_proj(a, w, kv_cols):
    m, k = a.shape
    n = w.shape[1]
    tm, tn = _tile(m, PROJ_TM), kv_cols
    assert (n - kv_cols) % tn == 0
    n_main = (n - kv_cols) // tn
    return pl.pallas_call(
        _in_proj_kernel,
        grid=(m // tm, n_main + 1),
        in_specs=[pl.BlockSpec((tm, k), lambda i, j: (i, 0)),
                  pl.BlockSpec((k, tn), lambda i, j: (0, jnp.where(j == 0, n_main, j - 1)))],
        out_specs=[pl.BlockSpec((tm, tn), lambda i, j: (i, jnp.maximum(j - 1, 0))),
                   pl.BlockSpec((tm, tn), lambda i, j: (i, 0))],
        out_shape=[jax.ShapeDtypeStruct((m, n - kv_cols), BF16), jax.ShapeDtypeStruct((m, kv_cols), F32)],
        compiler_params=_params(40, 2),
        name="in_proj",
    )(a, w)


def _mm_kernel(a_ref, w_ref, o_ref):
    o_ref[...] = jnp.dot(a_ref[...], w_ref[...], preferred_element_type=F32).astype(o_ref.dtype)


def _mm2_kernel(a1_ref, a2_ref, w1_ref, w2_ref, o_ref):
    o_ref[...] = (jnp.dot(a1_ref[...], w1_ref[...], preferred_element_type=F32)
                  + jnp.dot(a2_ref[...], w2_ref[...], preferred_element_type=F32)).astype(o_ref.dtype)


def _proj(acts, w, tm, tn, vmem_mib, name, casts=()):
    m, k = acts[0].shape
    assert all(a.shape == (m, k) for a in acts) and w.shape[0] == k * len(acts)
    n = w.shape[1]
    tm, tn = _tile(m, tm), _tile(n, tn)
    nj = n // tn
    n_steps = (m // tm) * nj
    cast_specs, cast_shapes = _cast_specs(casts, n_steps, lambda i, j: i * nj + j)
    body = _mm_kernel if len(acts) == 1 else _mm2_kernel
    out = pl.pallas_call(
        _with_casts(body, 2 * len(acts), 1, len(casts)),
        grid=(m // tm, nj),
        in_specs=([pl.BlockSpec((tm, k), lambda i, j: (i, 0))] * len(acts)
                  + [pl.BlockSpec((k, tn), lambda i, j, r=r: (r, j)) for r in range(len(acts))]
                  + cast_specs),
        out_specs=[pl.BlockSpec((tm, tn), lambda i, j: (i, j))] + cast_specs,
        out_shape=[jax.ShapeDtypeStruct((m, n), BF16)] + cast_shapes,
        compiler_params=_params(vmem_mib, 2),
        name=name,
    )(*acts, *([w] * len(acts)), *casts)
    return out[0], out[1:]


def _sgu_kernel(u_ref, v_ref, w_ref, bt_ref, g_ref, o_ref, *rest, n, groups, with_vn):
    if with_vn:
        vn_ref, acc_ref = rest
    else:
        (acc_ref,) = rest
    rows, d_a = u_ref.shape
    gd = d_a // groups
    qi = lax.broadcasted_iota(jnp.int32, (n, n), 0) // CHUNK
    kj = lax.broadcasted_iota(jnp.int32, (n, n), 1) // CHUNK
    causal = kj <= qi
    ssq = [jnp.zeros((n, 1), F32) for _ in range(rows // n)]
    for g in range(groups):
        wg = jnp.where(causal, w_ref[g], 0.0).astype(BF16)
        bias = bt_ref[:, g:g + 1]
        cols = slice(g * gd, (g + 1) * gd)
        for sb in range(rows // n):
            rs = slice(sb * n, (sb + 1) * n)
            vn = _rms(v_ref[rs, cols].astype(F32))
            if with_vn:
                vn_ref[rs, cols] = vn
            s = jnp.dot(wg, vn.astype(BF16), preferred_element_type=F32) + bias
            out = u_ref[rs, cols].astype(F32) * s
            acc_ref[rs, cols] = out
            ssq[sb] = ssq[sb] + jnp.sum(out * out, axis=-1, keepdims=True)
    for sb in range(rows // n):
        rs = slice(sb * n, (sb + 1) * n)
        inv = lax.rsqrt(ssq[sb] / d_a + EPS)
        o_ref[rs, :] = (acc_ref[rs, :] * inv * g_ref[...]).astype(BF16)


def _sgu(z, w_s, b_s, g_out, n, with_vn):
    m = z.shape[0]
    d_a = g_out.shape[0]
    groups = w_s.shape[0]
    rows = _tile(m, max(n, SGU_ROWS))
    out_specs = [pl.BlockSpec((rows, d_a), lambda i: (i, 0))]
    out_shape = [jax.ShapeDtypeStruct((m, d_a), BF16)]
    if with_vn:
        out_specs.append(pl.BlockSpec((rows, d_a), lambda i: (i, 0)))
        out_shape.append(jax.ShapeDtypeStruct((m, d_a), F32))
    return pl.pallas_call(
        functools.partial(_sgu_kernel, n=n, groups=groups, with_vn=with_vn),
        grid=(m // rows,),
        in_specs=[pl.BlockSpec((rows, d_a), lambda i: (i, 0)),
                  pl.BlockSpec((rows, d_a), lambda i: (i, 1)),
                  pl.BlockSpec((groups, n, n), lambda i: (0, 0, 0)),
                  pl.BlockSpec((n, groups), lambda i: (0, 0)),
                  pl.BlockSpec((1, d_a), lambda i: (0, 0))],
        out_specs=out_specs, out_shape=out_shape,
        scratch_shapes=[pltpu.VMEM((rows, d_a), F32)],
        compiler_params=_params(40, 1),
        name="sgu_mixer",
    )(z, z, w_s[:, :n, :n], b_s[:, :n].T, g_out.reshape(1, d_a))


def _attend(q_ref, kw, vw, bias, sink_ref, g_ref, o_ref, acc_ref, *, n_kv, hd):
    tq, d_b = q_ref.shape
    span = kw.shape[0]
    rep = d_b // (n_kv * hd)
    assert 2 * hd == LANES and rep % 2 == 0
    kscale = hd ** -0.5 * LOG2E
    zeros = jnp.zeros((span, hd), BF16)
    ones = jnp.ones((span, hd), BF16)
    first = lax.broadcasted_iota(jnp.int32, (tq, LANES), 1) < hd
    nt = (((1,), (1,)), ((), ()))
    for g in range(n_kv):
        kg = (kw[:, g * hd:(g + 1) * hd] * kscale).astype(BF16)
        vg = vw[:, g * hd:(g + 1) * hd].astype(BF16)
        keys = (jnp.concatenate([kg, zeros], axis=1), jnp.concatenate([zeros, kg], axis=1))
        vals = (jnp.concatenate([vg, zeros, ones, zeros], axis=1),
                jnp.concatenate([zeros, vg, zeros, ones], axis=1))
        for p in range(rep // 2):
            h0 = g * rep + 2 * p
            cols = slice(h0 * hd, (h0 + 2) * hd)
            qp = q_ref[:, cols]
            out, sink_e = None, []
            for i in range(2):
                s = lax.dot_general(qp, keys[i], nt, preferred_element_type=F32)
                if bias is not None:
                    s = s + bias
                sk = sink_ref[h0 + i] * LOG2E
                mx = jnp.maximum(jnp.max(s, axis=-1, keepdims=True), sk)
                e = jnp.exp2(s - mx).astype(BF16)
                sink_e.append(jnp.exp2(sk - mx))
                o = jnp.dot(e, vals[i], preferred_element_type=F32)
                out = o if out is None else out + o
            den = out[:, LANES:] + jnp.where(first, sink_e[0], sink_e[1])
            acc_ref[:, cols] = out[:, :LANES] / den
    o_ref[...] = (_rms(acc_ref[...]) * g_ref[...]).astype(BF16)


def _attn_prompt_kernel(q_ref, k_ref, v_ref, sink_ref, g_ref, o_ref, acc_ref, *, window, n_kv, hd):
    tq = q_ref.shape[0]
    span = tq + window
    q0 = pl.program_id(1) * tq
    start = pl.multiple_of(jnp.maximum(q0 - window, 0), CHUNK)
    kw = k_ref[pl.ds(start, span), :]
    vw = v_ref[pl.ds(start, span), :]
    qc = (q0 + lax.broadcasted_iota(jnp.int32, (tq, span), 0)) // CHUNK
    kc = (start + lax.broadcasted_iota(jnp.int32, (tq, span), 1)) // CHUNK
    bias = jnp.where(kc <= qc, jnp.where(kc >= qc - window // CHUNK, 0.0, NEG), NEG)
    _attend(q_ref, kw, vw, bias, sink_ref, g_ref, o_ref, acc_ref, n_kv=n_kv, hd=hd)


def _attn_prompt(zm3, zkv3, sink, g_out, window, n_kv, hd):
    bsz, s, _ = zm3.shape
    d_b = g_out.shape[0]
    kvw = n_kv * hd
    tq = _tile(s, ATTN_TQ)
    assert tq % CHUNK == 0 and window % CHUNK == 0 and tq + window <= s
    return pl.pallas_call(
        functools.partial(_attn_prompt_kernel, window=window, n_kv=n_kv, hd=hd),
        grid=(bsz, s // tq),
        in_specs=[pl.BlockSpec((None, tq, d_b), lambda b, t: (b, t, 2)),
                  pl.BlockSpec((None, s, kvw), lambda b, t: (b, 0, 0)),
                  pl.BlockSpec((None, s, kvw), lambda b, t: (b, 0, 1)),
                  pl.BlockSpec(memory_space=pltpu.SMEM),
                  pl.BlockSpec((1, d_b), lambda b, t: (0, 0))],
        out_specs=pl.BlockSpec((None, tq, d_b), lambda b, t: (b, t, 0)),
        out_shape=jax.ShapeDtypeStruct((bsz, s, d_b), BF16),
        scratch_shapes=[pltpu.VMEM((tq, d_b), F32)],
        compiler_params=_params(40, 2),
        name="attn_prompt",
    )(zm3, zkv3, zkv3, sink, g_out.reshape(1, d_b))


def _attn_sample_kernel(q_ref, kn_ref, vn_ref, ck_ref, cv_ref, sink_ref, g_ref, o_ref, acc_ref, *, n_kv, hd):
    kw = jnp.concatenate([ck_ref[...], kn_ref[...]], axis=0)
    vw = jnp.concatenate([cv_ref[...], vn_ref[...]], axis=0)
    _attend(q_ref, kw, vw, None, sink_ref, g_ref, o_ref, acc_ref, n_kv=n_kv, hd=hd)


def _attn_sample(zm3, zkv3, ck, cv, sink, g_out, n_kv, hd):
    bsz, t, _ = zm3.shape
    d_b = g_out.shape[0]
    kvw = n_kv * hd
    win = ck.shape[1]
    return pl.pallas_call(
        functools.partial(_attn_sample_kernel, n_kv=n_kv, hd=hd),
        grid=(bsz,),
        in_specs=[pl.BlockSpec((None, t, d_b), lambda b: (b, 0, 2)),
                  pl.BlockSpec((None, t, kvw), lambda b: (b, 0, 0)),
                  pl.BlockSpec((None, t, kvw), lambda b: (b, 0, 1)),
                  pl.BlockSpec((None, win, kvw), lambda b: (b, 0, 0)),
                  pl.BlockSpec((None, win, kvw), lambda b: (b, 0, 0)),
                  pl.BlockSpec(memory_space=pltpu.SMEM),
                  pl.BlockSpec((1, d_b), lambda b: (0, 0))],
        out_specs=pl.BlockSpec((None, t, d_b), lambda b: (b, 0, 0)),
        out_shape=jax.ShapeDtypeStruct((bsz, t, d_b), BF16),
        scratch_shapes=[pltpu.VMEM((t, d_b), F32)],
        compiler_params=_params(32, 1),
        name="attn_sample",
    )(zm3, zkv3, zkv3, ck.reshape(bsz, win, kvw), cv.reshape(bsz, win, kvw), sink, g_out.reshape(1, d_b))


def _conv_gate(a, b, prev, cw, cb):
    rows = a.shape[0]
    row = lax.broadcasted_iota(jnp.int32, a.shape, 0)
    a1 = jnp.where(row == 0, prev[1:2], pltpu.roll(a, 1, 0))
    a2 = jnp.where(row == 0, prev[0:1], jnp.where(row == 1, prev[1:2], pltpu.roll(a, 2, 0)))
    conv = cw[0:1] * a2 + cw[1:2] * a1 + cw[2:3] * a + cb
    return (jax.nn.silu(conv) * b).astype(BF16), a[rows - 2:rows]


def _up_prompt_kernel(h_ref, wa_ref, wb_ref, cw_ref, cb_ref, act_ref, nc_ref):
    tm = h_ref.shape[0]
    cw, cb = cw_ref[...], cb_ref[...]
    last = jnp.zeros((2, wa_ref.shape[1]), F32)
    rc = _tile(tm, UP_RC)
    for c in range(tm // rc):
        rs = slice(c * rc, (c + 1) * rc)
        h = h_ref[rs, :]
        a = jnp.dot(h, wa_ref[...], preferred_element_type=F32)
        b = jnp.dot(h, wb_ref[...], preferred_element_type=F32)
        act_ref[rs, :], last = _conv_gate(a, b, last, cw, cb)
    nc_ref[...] = last


def _up_proj_prompt(h, w_up, conv_w, conv_b, bsz, casts=()):
    m, d = h.shape
    d_ff = conv_w.shape[1]
    tm = m // bsz
    tn = _tile(d_ff, UP_TN)
    nj = d_ff // tn
    cast_specs, cast_shapes = _cast_specs(casts, bsz * nj, lambda i, j: i * nj + j)
    out = pl.pallas_call(
        _with_casts(_up_prompt_kernel, 5, 2, len(casts)),
        grid=(bsz, nj),
        in_specs=[pl.BlockSpec((tm, d), lambda i, j: (i, 0)),
                  pl.BlockSpec((d, tn), lambda i, j: (0, j)),
                  pl.BlockSpec((d, tn), lambda i, j: (0, nj + j)),
                  pl.BlockSpec((conv_w.shape[0], tn), lambda i, j: (0, j)),
                  pl.BlockSpec((1, tn), lambda i, j: (0, j))] + cast_specs,
        out_specs=[pl.BlockSpec((tm, tn), lambda i, j: (i, j)),
                   pl.BlockSpec((None, 2, tn), lambda i, j: (i, 0, j))] + cast_specs,
        out_shape=[jax.ShapeDtypeStruct((m, d_ff), BF16),
                   jax.ShapeDtypeStruct((bsz, 2, d_ff), F32)] + cast_shapes,
        compiler_params=_params(56, 2),
        name="up_proj_conv_gate",
    )(h, w_up, w_up, conv_w, conv_b.reshape(1, d_ff), *casts)
    return out[0], out[1], out[2:]


def _up_sample_kernel(h_ref, wa_ref, wb_ref, cw_ref, cb_ref, st_ref, act_ref, nc_ref):
    h = h_ref[...]
    a = jnp.dot(h, wa_ref[...], preferred_element_type=F32)
    b = jnp.dot(h, wb_ref[...], preferred_element_type=F32)
    cw, cb = cw_ref[...], cb_ref[...]
    pieces = st_ref.shape[0]
    t = h.shape[0] // pieces
    for p in range(pieces):
        rs = slice(p * t, (p + 1) * t)
        act_ref[rs, :], nc_ref[p] = _conv_gate(a[rs], b[rs], st_ref[p], cw, cb)


def _up_proj_sample(h, w_up, conv_w, conv_b, state):
    m, d = h.shape
    bsz = state.shape[0]
    d_ff = conv_w.shape[1]
    tn = _tile(d_ff, UP_TN)
    nj = d_ff // tn
    return pl.pallas_call(
        _up_sample_kernel,
        grid=(nj,),
        in_specs=[pl.BlockSpec((m, d), lambda j: (0, 0)),
                  pl.BlockSpec((d, tn), lambda j: (0, j)),
                  pl.BlockSpec((d, tn), lambda j: (0, nj + j)),
                  pl.BlockSpec((conv_w.shape[0], tn), lambda j: (0, j)),
                  pl.BlockSpec((1, tn), lambda j: (0, j)),
                  pl.BlockSpec((bsz, 2, tn), lambda j: (0, 0, j))],
        out_specs=[pl.BlockSpec((m, tn), lambda j: (0, j)),
                   pl.BlockSpec((bsz, 2, tn), lambda j: (0, 0, j))],
        out_shape=[jax.ShapeDtypeStruct((m, d_ff), BF16), jax.ShapeDtypeStruct((bsz, 2, d_ff), F32)],
        compiler_params=_params(32, 1),
        name="up_proj_conv_gate_sample",
    )(h, w_up, w_up, conv_w, conv_b.reshape(1, d_ff), state)


class _Weights:
    def __init__(self, depth):
        self.w_in = [None] * depth
        self.w_out = [None] * depth
        self.w_up = [None] * depth
        self.w_down = [None] * depth


def _trunk(x, mods, boff, p, wb, raw, cache, state):
    bsz, seq, d = x.shape
    depth = p["w_in"].shape[0]
    m = bsz * seq
    d_a, d_b = p["g_out_a"].shape[1], p["g_out_b"].shape[1]
    window, n_kv, hd = cache[0].shape[2:]
    kvw = n_kv * hd
    sample = raw is None
    n_blk = min(seq, p["sgu_w"].shape[2])
    ks, vs, convs, sgus = [], [], [], []
    h = _prenorm(x, p["g_pre_mix"][0], mods[0], boff, 1, 0)
    for l in range(depth):
        zm, zkv = _in_proj(h.reshape(m, d), wb.w_in[l], 2 * kvw)
        zm3, zkv3 = zm.reshape(bsz, seq, 3 * d_b), zkv.reshape(bsz, seq, 2 * kvw)
        k = zkv3[:, :, :kvw].reshape(bsz, seq, n_kv, hd)
        v = zkv3[:, :, kvw:].reshape(bsz, seq, n_kv, hd)
        if sample:
            a_n, vn = _sgu(zm, p["sgu_w"][l], p["sgu_b"][l], p["g_out_a"][l], n_blk, True)
            b_n = _attn_sample(zm3, zkv3, cache[0][l], cache[1][l], p["attn_sink"][l], p["g_out_b"][l], n_kv, hd)
            ks.append(k)
            vs.append(v)
            sgus.append(vn.reshape(bsz, seq, d_a))
        else:
            (a_n,) = _sgu(zm, p["sgu_w"][l], p["sgu_b"][l], p["g_out_a"][l], n_blk, False)
            b_n = _attn_prompt(zm3, zkv3, p["attn_sink"][l], p["g_out_b"][l], window, n_kv, hd)
            ks.append(k[:, -window:])
            vs.append(v[:, -window:])
        mix, done = _proj([a_n, b_n.reshape(m, d_b)], wb.w_out[l], OUT_TM, OUT_TN, 48, "out_proj",
                          () if sample else (raw[2][l],))
        if not sample:
            (wb.w_up[l],) = done
        x, h = _resid(x, mix, p["g_post_mix"][l], mods[l], boff, 2, (p["g_pre_ffn"][l], mods[l], 4, 3))
        if sample:
            act, new_conv = _up_proj_sample(h.reshape(m, d), wb.w_up[l], p["conv_w"][l], p["conv_b"][l], state[l])
        else:
            act, new_conv, (wb.w_down[l],) = _up_proj_prompt(
                h.reshape(m, d), wb.w_up[l], p["conv_w"][l], p["conv_b"][l], bsz, (raw[3][l],))
        convs.append(new_conv)
        nxt_casts = () if sample or l + 1 == depth else (raw[0][l + 1], raw[1][l + 1])
        f, done = _proj([act], wb.w_down[l], DOWN_TM, DOWN_TN, 56, "down_proj", nxt_casts)
        if nxt_casts:
            wb.w_in[l + 1], wb.w_out[l + 1] = done
        nxt = (p["g_pre_mix"][l + 1], mods[l + 1], 1, 0) if l + 1 < depth else None
        x, h = _resid(x, f, p["g_post_ffn"][l], mods[l], boff, 5, nxt)
    sgu = jnp.stack(sgus) if sample else None
    return x, jnp.stack(ks), jnp.stack(vs), jnp.stack(convs), sgu


def kernel(x_prompt, x_sample, c_prompt, c_sample, cache_k, cache_v, state_conv, w_mod, b_mod, g_pre_mix, g_post_mix, w_in, sgu_w, sgu_b, attn_sink, g_out_a, g_out_b, w_out, g_pre_ffn, g_post_ffn, w_up, conv_w, conv_b, w_down):
    depth, d = g_pre_mix.shape
    nb_p = c_prompt.shape[0]
    c_all = jnp.concatenate([c_prompt, c_sample], axis=0)
    mod = _modulation(c_all, w_mod, b_mod).reshape(depth, c_all.shape[0], N_MOD, 1, d)
    mods = [mod[l] for l in range(depth)]
    p = dict(g_pre_mix=g_pre_mix, g_post_mix=g_post_mix, w_in=w_in, sgu_w=sgu_w, sgu_b=sgu_b,
             attn_sink=attn_sink, g_out_a=g_out_a, g_out_b=g_out_b, g_pre_ffn=g_pre_ffn,
             g_post_ffn=g_post_ffn, conv_w=conv_w, conv_b=conv_b)
    wb = _Weights(depth)
    wb.w_in[0] = w_in[0].astype(BF16)
    wb.w_out[0] = w_out[0].astype(BF16)
    cache = (cache_k, cache_v)
    y_p, k_p, v_p, conv_p, _ = _trunk(x_prompt, mods, 0, p, wb, (w_in, w_out, w_up, w_down), cache, None)
    y_s, k_s, v_s, conv_s, sgu_s = _trunk(x_sample, mods, nb_p, p, wb, None, cache, state_conv)
    return (y_p, y_s, k_p, v_p, k_s, v_s, conv_p, conv_s, sgu_s)
```

```python
import functools

import jax
import jax.numpy as jnp
from jax import lax
from jax.experimental import pallas as pl
from jax.experimental.pallas import tpu as pltpu

F32 = jnp.float32
BF16 = jnp.bfloat16

CHUNK = 64
EPS = 1e-6
NEG = -1e30
N_MOD = 6
LOG2E = 1.4426950408889634
LANES = 128
BF16_ROWS = 16

MOD_TN = 512
NORM_TS = 256
PROJ_TM, PROJ_TN = 1024, 512
OUT_TM, OUT_TN = 1024, 1024
OUT_TN_HOST = 512
UP_TN = 256
UP_RC = 256
DOWN_TM, DOWN_TN = 512, 512
SGU_ROWS = 256
ATTN_TQ = 128
MIB = 1024 * 1024


def _tile(full, pref):
    t = min(full, pref)
    assert full % t == 0, (full, pref)
    return t


def _params(vmem_mib, n_axes):
    return pltpu.CompilerParams(dimension_semantics=("arbitrary",) * n_axes,
                                vmem_limit_bytes=vmem_mib * MIB)


def _rms(x):
    return x * lax.rsqrt(jnp.mean(x * x, axis=-1, keepdims=True) + EPS)


def _with_casts(body, n_in, n_out, n_cast):
    if not n_cast:
        return body

    def kernel(*refs):
        ins = refs[:n_in]
        cast_in = refs[n_in:n_in + n_cast]
        outs = refs[n_in + n_cast:n_in + n_cast + n_out]
        cast_out = refs[n_in + n_cast + n_out:n_in + 2 * n_cast + n_out]
        for src, dst in zip(cast_in, cast_out):
            dst[...] = src[...].astype(BF16)
        body(*ins, *outs, *refs[n_in + 2 * n_cast + n_out:])

    return kernel


def _cast_specs(casts, n_steps, step_of):
    in_specs, out_specs, shapes = [], [], []
    for w, l in casts:
        _, rows, cols = w.shape
        slab = rows // n_steps
        assert slab * n_steps == rows and slab % BF16_ROWS == 0, (w.shape, n_steps)
        in_specs.append(pl.BlockSpec((None, slab, cols), lambda *g, l=l: (l, step_of(*g), 0)))
        out_specs.append(pl.BlockSpec((slab, cols), lambda *g: (step_of(*g), 0)))
        shapes.append(jax.ShapeDtypeStruct((rows, cols), BF16))
    return in_specs, out_specs, shapes


def _mod_kernel(c_ref, w_ref, b_ref, o_ref):
    act = jax.nn.silu(c_ref[...]).astype(BF16)
    o_ref[...] = jnp.dot(act, w_ref[...].astype(BF16), preferred_element_type=F32) + b_ref[...]


def _modulation(c, w_mod, b_mod):
    depth, d, n = w_mod.shape
    rows = c.shape[0]
    tn = _tile(n, MOD_TN)
    return pl.pallas_call(
        _mod_kernel,
        grid=(depth, n // tn),
        in_specs=[pl.BlockSpec((rows, d), lambda l, j: (0, 0)),
                  pl.BlockSpec((None, d, tn), lambda l, j: (l, 0, j)),
                  pl.BlockSpec((None, 1, tn), lambda l, j: (l, 0, j))],
        out_specs=pl.BlockSpec((None, rows, tn), lambda l, j: (l, 0, j)),
        out_shape=jax.ShapeDtypeStruct((depth, rows, n), F32),
        compiler_params=_params(40, 2),
        name="modulation",
    )(c, w_mod, b_mod.reshape(depth, 1, n))


def _prenorm_kernel(x_ref, g_ref, sc_ref, sh_ref, h_ref):
    y = _rms(x_ref[...]) * g_ref[...]
    h_ref[...] = (y * (1 + sc_ref[...]) + sh_ref[...]).astype(BF16)


def _mod_spec(boff, j, d):
    return pl.BlockSpec((None, None, 1, d), lambda b, t: (boff + b, j, 0, 0))


def _prenorm(x, g, mod, boff, j_sc, j_sh):
    bsz, s, d = x.shape
    ts = _tile(s, NORM_TS)
    return pl.pallas_call(
        _prenorm_kernel,
        grid=(bsz, s // ts),
        in_specs=[pl.BlockSpec((None, ts, d), lambda b, t: (b, t, 0)),
                  pl.BlockSpec((1, d), lambda b, t: (0, 0)),
                  _mod_spec(boff, j_sc, d), _mod_spec(boff, j_sh, d)],
        out_specs=pl.BlockSpec((None, ts, d), lambda b, t: (b, t, 0)),
        out_shape=jax.ShapeDtypeStruct((bsz, s, d), BF16),
        compiler_params=_params(32, 2),
        name="prenorm",
    )(x, g.reshape(1, d), mod, mod)


def _resid_kernel(x_ref, m_ref, gt_ref, gpost_ref, *rest, with_next):
    xn = x_ref[...] + gt_ref[...] * (_rms(m_ref[...].astype(F32)) * gpost_ref[...])
    if with_next:
        gpre_ref, sc_ref, sh_ref, xo_ref, h_ref = rest
        y = _rms(xn) * gpre_ref[...]
        h_ref[...] = (y * (1 + sc_ref[...]) + sh_ref[...]).astype(BF16)
    else:
        (xo_ref,) = rest
    xo_ref[...] = xn


def _resid(x, m, g_post, mod, boff, j_gt, nxt=None):
    bsz, s, d = x.shape
    ts = _tile(s, NORM_TS)
    row = pl.BlockSpec((None, ts, d), lambda b, t: (b, t, 0))
    vec = pl.BlockSpec((1, d), lambda b, t: (0, 0))
    in_specs = [row, row, _mod_spec(boff, j_gt, d), vec]
    args = [x, m.reshape(bsz, s, d), mod, g_post.reshape(1, d)]
    out_specs = [row]
    out_shape = [jax.ShapeDtypeStruct((bsz, s, d), F32)]
    if nxt is not None:
        g_pre, mod_next, j_sc, j_sh = nxt
        in_specs += [vec, _mod_spec(boff, j_sc, d), _mod_spec(boff, j_sh, d)]
        args += [g_pre.reshape(1, d), mod_next, mod_next]
        out_specs.append(row)
        out_shape.append(jax.ShapeDtypeStruct((bsz, s, d), BF16))
    out = pl.pallas_call(
        functools.partial(_resid_kernel, with_next=nxt is not None),
        grid=(bsz, s // ts),
        in_specs=in_specs, out_specs=out_specs, out_shape=out_shape,
        compiler_params=_params(48, 2),
        name="resid_norm",
    )(*args)
    return out if nxt is not None else (out[0], None)


def _in_proj_kernel(a_ref, w_ref, zm_ref, zkv_ref):
    r = jnp.dot(a_ref[...], w_ref[...], preferred_element_type=F32)
    zm_ref[...] = r.astype(BF16)

    @pl.when(pl.program_id(1) == 0)
    def _():
        zkv_ref[...] = r


def _in_proj(a, w, kv_cols):
    m, k = a.shape
    n = w.shape[1]
    tm, tn = _tile(m, PROJ_TM), kv_cols
    assert (n - kv_cols) % tn == 0
    n_main = (n - kv_cols) // tn
    return pl.pallas_call(
        _in_proj_kernel,
        grid=(m // tm, n_main + 1),
        in_specs=[pl.BlockSpec((tm, k), lambda i, j: (i, 0)),
                  pl.BlockSpec((k, tn), lambda i, j: (0, jnp.where(j == 0, n_main, j - 1)))],
        out_specs=[pl.BlockSpec((tm, tn), lambda i, j: (i, jnp.maximum(j - 1, 0))),
                   pl.BlockSpec((tm, tn), lambda i, j: (i, 0))],
        out_shape=[jax.ShapeDtypeStruct((m, n - kv_cols), BF16), jax.ShapeDtypeStruct((m, kv_cols), F32)],
        compiler_params=_params(40, 2),
        name="in_proj",
    )(a, w)


def _mm_kernel(a_ref, w_ref, o_ref):
    o_ref[...] = jnp.dot(a_ref[...], w_ref[...], preferred_element_type=F32).astype(o_ref.dtype)


def _mm2_kernel(a1_ref, a2_ref, w1_ref, w2_ref, o_ref):
    o_ref[...] = (jnp.dot(a1_ref[...], w1_ref[...], preferred_element_type=F32)
                  + jnp.dot(a2_ref[...], w2_ref[...], preferred_element_type=F32)).astype(o_ref.dtype)


def _proj(acts, w, tm, tn, vmem_mib, name, casts=()):
    m, k = acts[0].shape
    assert all(a.shape == (m, k) for a in acts) and w.shape[0] == k * len(acts)
    n = w.shape[1]
    tm, tn = _tile(m, tm), _tile(n, tn)
    nj = n // tn
    n_steps = (m // tm) * nj
    cast_in, cast_out, cast_shapes = _cast_specs(casts, n_steps, lambda i, j: i * nj + j)
    body = _mm_kernel if len(acts) == 1 else _mm2_kernel
    out = pl.pallas_call(
        _with_casts(body, 2 * len(acts), 1, len(casts)),
        grid=(m // tm, nj),
        in_specs=([pl.BlockSpec((tm, k), lambda i, j: (i, 0))] * len(acts)
                  + [pl.BlockSpec((k, tn), lambda i, j, r=r: (r, j)) for r in range(len(acts))]
                  + cast_in),
        out_specs=[pl.BlockSpec((tm, tn), lambda i, j: (i, j))] + cast_out,
        out_shape=[jax.ShapeDtypeStruct((m, n), BF16)] + cast_shapes,
        compiler_params=_params(vmem_mib, 2),
        name=name,
    )(*acts, *([w] * len(acts)), *[c[0] for c in casts])
    return out[0], out[1:]


def _sgu_kernel(u_ref, v_ref, w_ref, bt_ref, g_ref, o_ref, *rest, n, groups, with_vn):
    if with_vn:
        vn_ref, acc_ref = rest
    else:
        (acc_ref,) = rest
    rows, d_a = u_ref.shape
    gd = d_a // groups
    qi = lax.broadcasted_iota(jnp.int32, (n, n), 0) // CHUNK
    kj = lax.broadcasted_iota(jnp.int32, (n, n), 1) // CHUNK
    causal = kj <= qi
    ssq = [jnp.zeros((n, 1), F32) for _ in range(rows // n)]
    for g in range(groups):
        wg = jnp.where(causal, w_ref[g], 0.0).astype(BF16)
        bias = bt_ref[:, g:g + 1]
        cols = slice(g * gd, (g + 1) * gd)
        for sb in range(rows // n):
            rs = slice(sb * n, (sb + 1) * n)
            vn = _rms(v_ref[rs, cols].astype(F32))
            if with_vn:
                vn_ref[rs, cols] = vn
            s = jnp.dot(wg, vn.astype(BF16), preferred_element_type=F32) + bias
            out = u_ref[rs, cols].astype(F32) * s
            acc_ref[rs, cols] = out
            ssq[sb] = ssq[sb] + jnp.sum(out * out, axis=-1, keepdims=True)
    for sb in range(rows // n):
        rs = slice(sb * n, (sb + 1) * n)
        inv = lax.rsqrt(ssq[sb] / d_a + EPS)
        o_ref[rs, :] = (acc_ref[rs, :] * inv * g_ref[...]).astype(BF16)


def _sgu(z, w_s, b_s, g_out, n, with_vn):
    m = z.shape[0]
    d_a = g_out.shape[0]
    groups = w_s.shape[0]
    rows = _tile(m, max(n, SGU_ROWS))
    out_specs = [pl.BlockSpec((rows, d_a), lambda i: (i, 0))]
    out_shape = [jax.ShapeDtypeStruct((m, d_a), BF16)]
    if with_vn:
        out_specs.append(pl.BlockSpec((rows, d_a), lambda i: (i, 0)))
        out_shape.append(jax.ShapeDtypeStruct((m, d_a), F32))
    return pl.pallas_call(
        functools.partial(_sgu_kernel, n=n, groups=groups, with_vn=with_vn),
        grid=(m // rows,),
        in_specs=[pl.BlockSpec((rows, d_a), lambda i: (i, 0)),
                  pl.BlockSpec((rows, d_a), lambda i: (i, 1)),
                  pl.BlockSpec((groups, n, n), lambda i: (0, 0, 0)),
                  pl.BlockSpec((n, groups), lambda i: (0, 0)),
                  pl.BlockSpec((1, d_a), lambda i: (0, 0))],
        out_specs=out_specs, out_shape=out_shape,
        scratch_shapes=[pltpu.VMEM((rows, d_a), F32)],
        compiler_params=_params(40, 1),
        name="sgu_mixer",
    )(z, z, w_s[:, :n, :n], b_s[:, :n].T, g_out.reshape(1, d_a))


def _attend(q_ref, kw, vw, bias, sink_ref, g_ref, o_ref, acc_ref, *, n_kv, hd):
    tq, d_b = q_ref.shape
    span = kw.shape[0]
    rep = d_b // (n_kv * hd)
    assert 2 * hd == LANES and rep % 2 == 0
    kscale = hd ** -0.5 * LOG2E
    zeros = jnp.zeros((span, hd), BF16)
    ones = jnp.ones((span, hd), BF16)
    first = lax.broadcasted_iota(jnp.int32, (tq, LANES), 1) < hd
    nt = (((1,), (1,)), ((), ()))
    for g in range(n_kv):
        kg = (kw[:, g * hd:(g + 1) * hd] * kscale).astype(BF16)
        vg = vw[:, g * hd:(g + 1) * hd].astype(BF16)
        keys = (jnp.concatenate([kg, zeros], axis=1), jnp.concatenate([zeros, kg], axis=1))
        vals = (jnp.concatenate([vg, zeros, ones, zeros], axis=1),
                jnp.concatenate([zeros, vg, zeros, ones], axis=1))
        for p in range(rep // 2):
            h0 = g * rep + 2 * p
            cols = slice(h0 * hd, (h0 + 2) * hd)
            qp = q_ref[:, cols]
            out, sink_e = None, []
            for i in range(2):
                s = lax.dot_general(qp, keys[i], nt, preferred_element_type=F32)
                if bias is not None:
                    s = s + bias
                sk = sink_ref[h0 + i] * LOG2E
                mx = jnp.maximum(jnp.max(s, axis=-1, keepdims=True), sk)
                e = jnp.exp2(s - mx).astype(BF16)
                sink_e.append(jnp.exp2(sk - mx))
                o = jnp.dot(e, vals[i], preferred_element_type=F32)
                out = o if out is None else out + o
            den = out[:, LANES:] + jnp.where(first, sink_e[0], sink_e[1])
            acc_ref[:, cols] = out[:, :LANES] / den
    o_ref[...] = (_rms(acc_ref[...]) * g_ref[...]).astype(BF16)


def _attn_prompt_kernel(q_ref, k_ref, v_ref, sink_ref, g_ref, o_ref, acc_ref, *, window, n_kv, hd):
    tq = q_ref.shape[0]
    span = tq + window
    q0 = pl.program_id(1) * tq
    start = pl.multiple_of(jnp.maximum(q0 - window, 0), CHUNK)
    kw = k_ref[pl.ds(start, span), :]
    vw = v_ref[pl.ds(start, span), :]
    qc = (q0 + lax.broadcasted_iota(jnp.int32, (tq, span), 0)) // CHUNK
    kc = (start + lax.broadcasted_iota(jnp.int32, (tq, span), 1)) // CHUNK
    bias = jnp.where(kc <= qc, jnp.where(kc >= qc - window // CHUNK, 0.0, NEG), NEG)
    _attend(q_ref, kw, vw, bias, sink_ref, g_ref, o_ref, acc_ref, n_kv=n_kv, hd=hd)


def _attn_prompt(zm3, zkv3, sink, g_out, window, n_kv, hd):
    bsz, s, _ = zm3.shape
    d_b = g_out.shape[0]
    kvw = n_kv * hd
    tq = _tile(s, ATTN_TQ)
    assert tq % CHUNK == 0 and window % CHUNK == 0 and tq + window <= s
    return pl.pallas_call(
        functools.partial(_attn_prompt_kernel, window=window, n_kv=n_kv, hd=hd),
        grid=(bsz, s // tq),
        in_specs=[pl.BlockSpec((None, tq, d_b), lambda b, t: (b, t, 2)),
                  pl.BlockSpec((None, s, kvw), lambda b, t: (b, 0, 0)),
                  pl.BlockSpec((None, s, kvw), lambda b, t: (b, 0, 1)),
                  pl.BlockSpec(memory_space=pltpu.SMEM),
                  pl.BlockSpec((1, d_b), lambda b, t: (0, 0))],
        out_specs=pl.BlockSpec((None, tq, d_b), lambda b, t: (b, t, 0)),
        out_shape=jax.ShapeDtypeStruct((bsz, s, d_b), BF16),
        scratch_shapes=[pltpu.VMEM((tq, d_b), F32)],
        compiler_params=_params(40, 2),
        name="attn_prompt",
    )(zm3, zkv3, zkv3, sink, g_out.reshape(1, d_b))


def _attn_sample_kernel(q_ref, kn_ref, vn_ref, ck_ref, cv_ref, sink_ref, g_ref, o_ref, acc_ref, *, n_kv, hd):
    kw = jnp.concatenate([ck_ref[...], kn_ref[...]], axis=0)
    vw = jnp.concatenate([cv_ref[...], vn_ref[...]], axis=0)
    _attend(q_ref, kw, vw, None, sink_ref, g_ref, o_ref, acc_ref, n_kv=n_kv, hd=hd)


def _attn_sample(zm3, zkv3, ck, cv, sink, g_out, n_kv, hd):
    bsz, t, _ = zm3.shape
    d_b = g_out.shape[0]
    kvw = n_kv * hd
    win = ck.shape[1]
    return pl.pallas_call(
        functools.partial(_attn_sample_kernel, n_kv=n_kv, hd=hd),
        grid=(bsz,),
        in_specs=[pl.BlockSpec((None, t, d_b), lambda b: (b, 0, 2)),
                  pl.BlockSpec((None, t, kvw), lambda b: (b, 0, 0)),
                  pl.BlockSpec((None, t, kvw), lambda b: (b, 0, 1)),
                  pl.BlockSpec((None, win, kvw), lambda b: (b, 0, 0)),
                  pl.BlockSpec((None, win, kvw), lambda b: (b, 0, 0)),
                  pl.BlockSpec(memory_space=pltpu.SMEM),
                  pl.BlockSpec((1, d_b), lambda b: (0, 0))],
        out_specs=pl.BlockSpec((None, t, d_b), lambda b: (b, 0, 0)),
        out_shape=jax.ShapeDtypeStruct((bsz, t, d_b), BF16),
        scratch_shapes=[pltpu.VMEM((t, d_b), F32)],
        compiler_params=_params(32, 1),
        name="attn_sample",
    )(zm3, zkv3, zkv3, ck.reshape(bsz, win, kvw), cv.reshape(bsz, win, kvw), sink, g_out.reshape(1, d_b))


def _conv_gate(a, b, prev, cw, cb):
    rows = a.shape[0]
    row = lax.broadcasted_iota(jnp.int32, a.shape, 0)
    a1 = jnp.where(row == 0, prev[1:2], pltpu.roll(a, 1, 0))
    a2 = jnp.where(row == 0, prev[0:1], jnp.where(row == 1, prev[1:2], pltpu.roll(a, 2, 0)))
    conv = cw[0:1] * a2 + cw[1:2] * a1 + cw[2:3] * a + cb
    return (jax.nn.silu(conv) * b).astype(BF16), a[rows - 2:rows]


def _up_prompt_kernel(h_ref, wa_ref, wb_ref, cw_ref, cb_ref, act_ref, nc_ref):
    tm = h_ref.shape[0]
    cw, cb = cw_ref[...], cb_ref[...]
    last = jnp.zeros((2, wa_ref.shape[1]), F32)
    rc = _tile(tm, UP_RC)
    for c in range(tm // rc):
        rs = slice(c * rc, (c + 1) * rc)
        h = h_ref[rs, :]
        a = jnp.dot(h, wa_ref[...], preferred_element_type=F32)
        b = jnp.dot(h, wb_ref[...], preferred_element_type=F32)
        act_ref[rs, :], last = _conv_gate(a, b, last, cw, cb)
    nc_ref[...] = last


def _up_proj_prompt(h, w_up, conv_w, conv_b, bsz, casts=()):
    m, d = h.shape
    d_ff = conv_w.shape[1]
    tm = m // bsz
    tn = _tile(d_ff, UP_TN)
    nj = d_ff // tn
    cast_in, cast_out, cast_shapes = _cast_specs(casts, bsz * nj, lambda i, j: i * nj + j)
    out = pl.pallas_call(
        _with_casts(_up_prompt_kernel, 5, 2, len(casts)),
        grid=(bsz, nj),
        in_specs=[pl.BlockSpec((tm, d), lambda i, j: (i, 0)),
                  pl.BlockSpec((d, tn), lambda i, j: (0, j)),
                  pl.BlockSpec((d, tn), lambda i, j: (0, nj + j)),
                  pl.BlockSpec((conv_w.shape[0], tn), lambda i, j: (0, j)),
                  pl.BlockSpec((1, tn), lambda i, j: (0, j))] + cast_in,
        out_specs=[pl.BlockSpec((tm, tn), lambda i, j: (i, j)),
                   pl.BlockSpec((None, 2, tn), lambda i, j: (i, 0, j))] + cast_out,
        out_shape=[jax.ShapeDtypeStruct((m, d_ff), BF16),
                   jax.ShapeDtypeStruct((bsz, 2, d_ff), F32)] + cast_shapes,
        compiler_params=_params(56, 2),
        name="up_proj_conv_gate",
    )(h, w_up, w_up, conv_w, conv_b.reshape(1, d_ff), *[c[0] for c in casts])
    return out[0], out[1], out[2:]


def _up_sample_kernel(h_ref, wa_ref, wb_ref, cw_ref, cb_ref, st_ref, act_ref, nc_ref):
    h = h_ref[...]
    a = jnp.dot(h, wa_ref[...], preferred_element_type=F32)
    b = jnp.dot(h, wb_ref[...], preferred_element_type=F32)
    cw, cb = cw_ref[...], cb_ref[...]
    pieces = st_ref.shape[0]
    t = h.shape[0] // pieces
    for p in range(pieces):
        rs = slice(p * t, (p + 1) * t)
        act_ref[rs, :], nc_ref[p] = _conv_gate(a[rs], b[rs], st_ref[p], cw, cb)


def _up_proj_sample(h, w_up, conv_w, conv_b, state):
    m, d = h.shape
    bsz = state.shape[0]
    d_ff = conv_w.shape[1]
    tn = _tile(d_ff, UP_TN)
    nj = d_ff // tn
    return pl.pallas_call(
        _up_sample_kernel,
        grid=(nj,),
        in_specs=[pl.BlockSpec((m, d), lambda j: (0, 0)),
                  pl.BlockSpec((d, tn), lambda j: (0, j)),
                  pl.BlockSpec((d, tn), lambda j: (0, nj + j)),
                  pl.BlockSpec((conv_w.shape[0], tn), lambda j: (0, j)),
                  pl.BlockSpec((1, tn), lambda j: (0, j)),
                  pl.BlockSpec((bsz, 2, tn), lambda j: (0, 0, j))],
        out_specs=[pl.BlockSpec((m, tn), lambda j: (0, j)),
                   pl.BlockSpec((bsz, 2, tn), lambda j: (0, 0, j))],
        out_shape=[jax.ShapeDtypeStruct((m, d_ff), BF16), jax.ShapeDtypeStruct((bsz, 2, d_ff), F32)],
        compiler_params=_params(32, 1),
        name="up_proj_conv_gate_sample",
    )(h, w_up, w_up, conv_w, conv_b.reshape(1, d_ff), state)


class _Weights:
    def __init__(self, depth):
        self.w_in = [None] * depth
        self.w_out = [None] * depth
        self.w_up = [None] * depth
        self.w_down = [None] * depth


def _trunk(x, mods, boff, p, wb, raw, cache, state):
    bsz, seq, d = x.shape
    depth = p["w_in"].shape[0]
    m = bsz * seq
    d_a, d_b = p["g_out_a"].shape[1], p["g_out_b"].shape[1]
    window, n_kv, hd = cache[0].shape[2:]
    kvw = n_kv * hd
    sample = raw is None
    n_blk = min(seq, p["sgu_w"].shape[2])
    ks, vs, convs, sgus = [], [], [], []
    h = _prenorm(x, p["g_pre_mix"][0], mods[0], boff, 1, 0)
    for l in range(depth):
        zm, zkv = _in_proj(h.reshape(m, d), wb.w_in[l], 2 * kvw)
        zm3, zkv3 = zm.reshape(bsz, seq, 3 * d_b), zkv.reshape(bsz, seq, 2 * kvw)
        k = zkv3[:, :, :kvw].reshape(bsz, seq, n_kv, hd)
        v = zkv3[:, :, kvw:].reshape(bsz, seq, n_kv, hd)
        if sample:
            a_n, vn = _sgu(zm, p["sgu_w"][l], p["sgu_b"][l], p["g_out_a"][l], n_blk, True)
            b_n = _attn_sample(zm3, zkv3, cache[0][l], cache[1][l], p["attn_sink"][l], p["g_out_b"][l], n_kv, hd)
            ks.append(k)
            vs.append(v)
            sgus.append(vn.reshape(bsz, seq, d_a))
        else:
            (a_n,) = _sgu(zm, p["sgu_w"][l], p["sgu_b"][l], p["g_out_a"][l], n_blk, False)
            b_n = _attn_prompt(zm3, zkv3, p["attn_sink"][l], p["g_out_b"][l], window, n_kv, hd)
            ks.append(k[:, -window:])
            vs.append(v[:, -window:])
        up_cast = ((raw[2], l),) if not sample and l == 0 else ()
        mix, done = _proj([a_n, b_n.reshape(m, d_b)], wb.w_out[l], OUT_TM,
                          OUT_TN_HOST if up_cast else OUT_TN, 48, "out_proj", up_cast)
        if up_cast:
            (wb.w_up[l],) = done
        x, h = _resid(x, mix, p["g_post_mix"][l], mods[l], boff, 2, (p["g_pre_ffn"][l], mods[l], 4, 3))
        if sample:
            act, new_conv = _up_proj_sample(h.reshape(m, d), wb.w_up[l], p["conv_w"][l], p["conv_b"][l], state[l])
        else:
            act, new_conv, (wb.w_down[l],) = _up_proj_prompt(
                h.reshape(m, d), wb.w_up[l], p["conv_w"][l], p["conv_b"][l], bsz, ((raw[3], l),))
        convs.append(new_conv)
        nxt_casts = () if sample or l + 1 == depth else tuple((w, l + 1) for w in raw[:3])
        f, done = _proj([act], wb.w_down[l], DOWN_TM, DOWN_TN, 56, "down_proj", nxt_casts)
        if nxt_casts:
            wb.w_in[l + 1], wb.w_out[l + 1], wb.w_up[l + 1] = done
        nxt = (p["g_pre_mix"][l + 1], mods[l + 1], 1, 0) if l + 1 < depth else None
        x, h = _resid(x, f, p["g_post_ffn"][l], mods[l], boff, 5, nxt)
    sgu = jnp.stack(sgus) if sample else None
    return x, jnp.stack(ks), jnp.stack(vs), jnp.stack(convs), sgu


def kernel(x_prompt, x_sample, c_prompt, c_sample, cache_k, cache_v, state_conv, w_mod, b_mod, g_pre_mix, g_post_mix, w_in, sgu_w, sgu_b, attn_sink, g_out_a, g_out_b, w_out, g_pre_ffn, g_post_ffn, w_up, conv_w, conv_b, w_down):
    depth, d = g_pre_mix.shape
    nb_p = c_prompt.shape[0]
    c_all = jnp.concatenate([c_prompt, c_sample], axis=0)
    mod = _modulation(c_all, w_mod, b_mod).reshape(depth, c_all.shape[0], N_MOD, 1, d)
    mods = [mod[l] for l in range(depth)]
    p = dict(g_pre_mix=g_pre_mix, g_post_mix=g_post_mix, w_in=w_in, sgu_w=sgu_w, sgu_b=sgu_b,
             attn_sink=attn_sink, g_out_a=g_out_a, g_out_b=g_out_b, g_pre_ffn=g_pre_ffn,
             g_post_ffn=g_post_ffn, conv_w=conv_w, conv_b=conv_b)
    wb = _Weights(depth)
    wb.w_in[0] = w_in[0].astype(BF16)
    wb.w_out[0] = w_out[0].astype(BF16)
    cache = (cache_k, cache_v)
    y_p, k_p, v_p, conv_p, _ = _trunk(x_prompt, mods, 0, p, wb, (w_in, w_out, w_up, w_down), cache, None)
    y_s, k_s, v_s, conv_s, sgu_s = _trunk(x_sample, mods, nb_p, p, wb, None, cache, state_conv)
    return (y_p, y_s, k_p, v_p, k_s, v_s, conv_p, conv_s, sgu_s)
```

```python
import functools

import jax
import jax.numpy as jnp
from jax import lax
from jax.experimental import pallas as pl
from jax.experimental.pallas import tpu as pltpu

F32 = jnp.float32
BF16 = jnp.bfloat16

CHUNK = 64
EPS = 1e-6
NEG = -1e30
N_MOD = 6
LOG2E = 1.4426950408889634
LANES = 128
BF16_ROWS = 16

MOD_TN = 512
NORM_TS = 256
PROJ_TM = 2048
OUT_TM, OUT_TN = 1024, 1024
OUT_TN_HOST = 512
UP_TN = 256
UP_RC = 256
DOWN_TM, DOWN_TN = 512, 512
SGU_ROWS = 256
ATTN_TQ = 128
MIB = 1024 * 1024


def _tile(full, pref):
    t = min(full, pref)
    assert full % t == 0, (full, pref)
    return t


def _params(vmem_mib, n_axes):
    return pltpu.CompilerParams(dimension_semantics=("arbitrary",) * n_axes,
                                vmem_limit_bytes=vmem_mib * MIB)


def _rms(x):
    return x * lax.rsqrt(jnp.mean(x * x, axis=-1, keepdims=True) + EPS)


def _with_casts(body, n_in, n_out, n_cast):
    if not n_cast:
        return body

    def kernel(*refs):
        ins = refs[:n_in]
        cast_in = refs[n_in:n_in + n_cast]
        outs = refs[n_in + n_cast:n_in + n_cast + n_out]
        cast_out = refs[n_in + n_cast + n_out:n_in + 2 * n_cast + n_out]
        for src, dst in zip(cast_in, cast_out):
            dst[...] = src[...].astype(BF16)
        body(*ins, *outs, *refs[n_in + 2 * n_cast + n_out:])

    return kernel


def _cast_specs(casts, n_steps, step_of):
    in_specs, out_specs, shapes = [], [], []
    for w, l in casts:
        _, rows, cols = w.shape
        slab = rows // n_steps
        assert slab * n_steps == rows and slab % BF16_ROWS == 0, (w.shape, n_steps)
        in_specs.append(pl.BlockSpec((None, slab, cols), lambda *g, l=l: (l, step_of(*g), 0)))
        out_specs.append(pl.BlockSpec((slab, cols), lambda *g: (step_of(*g), 0)))
        shapes.append(jax.ShapeDtypeStruct((rows, cols), BF16))
    return in_specs, out_specs, shapes


def _mod_kernel(c_ref, w_ref, b_ref, o_ref):
    act = jax.nn.silu(c_ref[...]).astype(BF16)
    o_ref[...] = jnp.dot(act, w_ref[...].astype(BF16), preferred_element_type=F32) + b_ref[...]


def _modulation(c, w_mod, b_mod):
    depth, d, n = w_mod.shape
    rows = c.shape[0]
    tn = _tile(n, MOD_TN)
    return pl.pallas_call(
        _mod_kernel,
        grid=(depth, n // tn),
        in_specs=[pl.BlockSpec((rows, d), lambda l, j: (0, 0)),
                  pl.BlockSpec((None, d, tn), lambda l, j: (l, 0, j)),
                  pl.BlockSpec((None, 1, tn), lambda l, j: (l, 0, j))],
        out_specs=pl.BlockSpec((None, rows, tn), lambda l, j: (l, 0, j)),
        out_shape=jax.ShapeDtypeStruct((depth, rows, n), F32),
        compiler_params=_params(40, 2),
        name="modulation",
    )(c, w_mod, b_mod.reshape(depth, 1, n))


def _prenorm_kernel(x_ref, g_ref, sc_ref, sh_ref, h_ref):
    y = _rms(x_ref[...]) * g_ref[...]
    h_ref[...] = (y * (1 + sc_ref[...]) + sh_ref[...]).astype(BF16)


def _mod_spec(boff, j, d):
    return pl.BlockSpec((None, None, 1, d), lambda b, t: (boff + b, j, 0, 0))


def _prenorm(x, g, mod, boff, j_sc, j_sh):
    bsz, s, d = x.shape
    ts = _tile(s, NORM_TS)
    return pl.pallas_call(
        _prenorm_kernel,
        grid=(bsz, s // ts),
        in_specs=[pl.BlockSpec((None, ts, d), lambda b, t: (b, t, 0)),
                  pl.BlockSpec((1, d), lambda b, t: (0, 0)),
                  _mod_spec(boff, j_sc, d), _mod_spec(boff, j_sh, d)],
        out_specs=pl.BlockSpec((None, ts, d), lambda b, t: (b, t, 0)),
        out_shape=jax.ShapeDtypeStruct((bsz, s, d), BF16),
        compiler_params=_params(32, 2),
        name="prenorm",
    )(x, g.reshape(1, d), mod, mod)


def _resid_kernel(x_ref, m_ref, gt_ref, gpost_ref, *rest, with_next):
    xn = x_ref[...] + gt_ref[...] * (_rms(m_ref[...].astype(F32)) * gpost_ref[...])
    if with_next:
        gpre_ref, sc_ref, sh_ref, xo_ref, h_ref = rest
        y = _rms(xn) * gpre_ref[...]
        h_ref[...] = (y * (1 + sc_ref[...]) + sh_ref[...]).astype(BF16)
    else:
        (xo_ref,) = rest
    xo_ref[...] = xn


def _resid(x, m, g_post, mod, boff, j_gt, nxt=None):
    bsz, s, d = x.shape
    ts = _tile(s, NORM_TS)
    row = pl.BlockSpec((None, ts, d), lambda b, t: (b, t, 0))
    vec = pl.BlockSpec((1, d), lambda b, t: (0, 0))
    in_specs = [row, row, _mod_spec(boff, j_gt, d), vec]
    args = [x, m.reshape(bsz, s, d), mod, g_post.reshape(1, d)]
    out_specs = [row]
    out_shape = [jax.ShapeDtypeStruct((bsz, s, d), F32)]
    if nxt is not None:
        g_pre, mod_next, j_sc, j_sh = nxt
        in_specs += [vec, _mod_spec(boff, j_sc, d), _mod_spec(boff, j_sh, d)]
        args += [g_pre.reshape(1, d), mod_next, mod_next]
        out_specs.append(row)
        out_shape.append(jax.ShapeDtypeStruct((bsz, s, d), BF16))
    out = pl.pallas_call(
        functools.partial(_resid_kernel, with_next=nxt is not None),
        grid=(bsz, s // ts),
        in_specs=in_specs, out_specs=out_specs, out_shape=out_shape,
        compiler_params=_params(48, 2),
        name="resid_norm",
    )(*args)
    return out if nxt is not None else (out[0], None)


def _in_proj_kernel(a_ref, w_ref, zm_ref, zkv_ref):
    r = jnp.dot(a_ref[...], w_ref[...], preferred_element_type=F32)
    zm_ref[...] = r.astype(BF16)

    @pl.when(pl.program_id(1) == 0)
    def _():
        zkv_ref[...] = r


def _in_proj(a, w, kv_cols):
    m, k = a.shape
    n = w.shape[1]
    tm, tn = _tile(m, PROJ_TM), kv_cols
    assert (n - kv_cols) % tn == 0
    n_main = (n - kv_cols) // tn
    return pl.pallas_call(
        _in_proj_kernel,
        grid=(m // tm, n_main + 1),
        in_specs=[pl.BlockSpec((tm, k), lambda i, j: (i, 0)),
                  pl.BlockSpec((k, tn), lambda i, j: (0, jnp.where(j == 0, n_main, j - 1)))],
        out_specs=[pl.BlockSpec((tm, tn), lambda i, j: (i, jnp.maximum(j - 1, 0))),
                   pl.BlockSpec((tm, tn), lambda i, j: (i, 0))],
        out_shape=[jax.ShapeDtypeStruct((m, n - kv_cols), BF16), jax.ShapeDtypeStruct((m, kv_cols), F32)],
        compiler_params=_params(56, 2),
        name="in_proj",
    )(a, w)


def _mm_kernel(a_ref, w_ref, o_ref):
    o_ref[...] = jnp.dot(a_ref[...], w_ref[...], preferred_element_type=F32).astype(o_ref.dtype)


def _mm2_kernel(a1_ref, a2_ref, w1_ref, w2_ref, o_ref):
    o_ref[...] = (jnp.dot(a1_ref[...], w1_ref[...], preferred_element_type=F32)
                  + jnp.dot(a2_ref[...], w2_ref[...], preferred_element_type=F32)).astype(o_ref.dtype)


def _proj(acts, w, tm, tn, vmem_mib, name, casts=()):
    m, k = acts[0].shape
    assert all(a.shape == (m, k) for a in acts) and w.shape[0] == k * len(acts)
    n = w.shape[1]
    tm, tn = _tile(m, tm), _tile(n, tn)
    nj = n // tn
    n_steps = (m // tm) * nj
    cast_in, cast_out, cast_shapes = _cast_specs(casts, n_steps, lambda i, j: i * nj + j)
    body = _mm_kernel if len(acts) == 1 else _mm2_kernel
    out = pl.pallas_call(
        _with_casts(body, 2 * len(acts), 1, len(casts)),
        grid=(m // tm, nj),
        in_specs=([pl.BlockSpec((tm, k), lambda i, j: (i, 0))] * len(acts)
                  + [pl.BlockSpec((k, tn), lambda i, j, r=r: (r, j)) for r in range(len(acts))]
                  + cast_in),
        out_specs=[pl.BlockSpec((tm, tn), lambda i, j: (i, j))] + cast_out,
        out_shape=[jax.ShapeDtypeStruct((m, n), BF16)] + cast_shapes,
        compiler_params=_params(vmem_mib, 2),
        name=name,
    )(*acts, *([w] * len(acts)), *[c[0] for c in casts])
    return out[0], out[1:]


def _sgu_kernel(u_ref, v_ref, w_ref, bt_ref, g_ref, o_ref, *rest, n, groups, with_vn):
    if with_vn:
        vn_ref, acc_ref = rest
    else:
        (acc_ref,) = rest
    rows, d_a = u_ref.shape
    gd = d_a // groups
    qi = lax.broadcasted_iota(jnp.int32, (n, n), 0) // CHUNK
    kj = lax.broadcasted_iota(jnp.int32, (n, n), 1) // CHUNK
    causal = kj <= qi
    sq = [jnp.zeros((n, gd), F32) for _ in range(rows // n)]
    for g in range(groups):
        wg = jnp.where(causal, w_ref[g], 0.0).astype(BF16)
        bias = bt_ref[:, g:g + 1]
        cols = slice(g * gd, (g + 1) * gd)
        for sb in range(rows // n):
            rs = slice(sb * n, (sb + 1) * n)
            vn = _rms(v_ref[rs, cols].astype(F32))
            if with_vn:
                vn_ref[rs, cols] = vn
            s = jnp.dot(wg, vn.astype(BF16), preferred_element_type=F32) + bias
            out = u_ref[rs, cols].astype(F32) * s
            acc_ref[rs, cols] = out
            sq[sb] = sq[sb] + out * out
    for sb in range(rows // n):
        rs = slice(sb * n, (sb + 1) * n)
        inv = lax.rsqrt(jnp.sum(sq[sb], axis=-1, keepdims=True) / d_a + EPS)
        o_ref[rs, :] = (acc_ref[rs, :] * inv * g_ref[...]).astype(BF16)


def _sgu(z, w_s, b_s, g_out, n, with_vn):
    m = z.shape[0]
    d_a = g_out.shape[0]
    groups = w_s.shape[0]
    rows = _tile(m, max(n, SGU_ROWS))
    out_specs = [pl.BlockSpec((rows, d_a), lambda i: (i, 0))]
    out_shape = [jax.ShapeDtypeStruct((m, d_a), BF16)]
    if with_vn:
        out_specs.append(pl.BlockSpec((rows, d_a), lambda i: (i, 0)))
        out_shape.append(jax.ShapeDtypeStruct((m, d_a), F32))
    return pl.pallas_call(
        functools.partial(_sgu_kernel, n=n, groups=groups, with_vn=with_vn),
        grid=(m // rows,),
        in_specs=[pl.BlockSpec((rows, d_a), lambda i: (i, 0)),
                  pl.BlockSpec((rows, d_a), lambda i: (i, 1)),
                  pl.BlockSpec((groups, n, n), lambda i: (0, 0, 0)),
                  pl.BlockSpec((n, groups), lambda i: (0, 0)),
                  pl.BlockSpec((1, d_a), lambda i: (0, 0))],
        out_specs=out_specs, out_shape=out_shape,
        scratch_shapes=[pltpu.VMEM((rows, d_a), F32)],
        compiler_params=_params(40, 1),
        name="sgu_mixer",
    )(z, z, w_s[:, :n, :n], b_s[:, :n].T, g_out.reshape(1, d_a))


def _attend(q_ref, kw, vw, bias, sink_ref, g_ref, o_ref, acc_ref, *, n_kv, hd):
    tq, d_b = q_ref.shape
    span = kw.shape[0]
    rep = d_b // (n_kv * hd)
    assert 2 * hd == LANES and rep % 2 == 0
    kscale = hd ** -0.5 * LOG2E
    zeros = jnp.zeros((span, hd), BF16)
    ones = jnp.ones((span, hd), BF16)
    first = lax.broadcasted_iota(jnp.int32, (tq, LANES), 1) < hd
    nt = (((1,), (1,)), ((), ()))
    for g in range(n_kv):
        kg = (kw[:, g * hd:(g + 1) * hd] * kscale).astype(BF16)
        vg = vw[:, g * hd:(g + 1) * hd].astype(BF16)
        keys = (jnp.concatenate([kg, zeros], axis=1), jnp.concatenate([zeros, kg], axis=1))
        vals = (jnp.concatenate([vg, zeros, ones, zeros], axis=1),
                jnp.concatenate([zeros, vg, zeros, ones], axis=1))
        for p in range(rep // 2):
            h0 = g * rep + 2 * p
            cols = slice(h0 * hd, (h0 + 2) * hd)
            qp = q_ref[:, cols]
            out, sink_e = None, []
            for i in range(2):
                s = lax.dot_general(qp, keys[i], nt, preferred_element_type=F32)
                if bias is not None:
                    s = s + bias
                sk = sink_ref[h0 + i] * LOG2E
                mx = jnp.maximum(jnp.max(s, axis=-1, keepdims=True), sk)
                e = jnp.exp2(s - mx).astype(BF16)
                sink_e.append(jnp.exp2(sk - mx))
                o = jnp.dot(e, vals[i], preferred_element_type=F32)
                out = o if out is None else out + o
            den = out[:, LANES:] + jnp.where(first, sink_e[0], sink_e[1])
            acc_ref[:, cols] = out[:, :LANES] / den
    o_ref[...] = (_rms(acc_ref[...]) * g_ref[...]).astype(BF16)


def _attn_prompt_kernel(q_ref, k_ref, v_ref, sink_ref, g_ref, o_ref, acc_ref, *, window, n_kv, hd):
    tq = q_ref.shape[0]
    span = tq + window
    q0 = pl.program_id(1) * tq
    start = pl.multiple_of(jnp.maximum(q0 - window, 0), CHUNK)
    kw = k_ref[pl.ds(start, span), :]
    vw = v_ref[pl.ds(start, span), :]
    qc = (q0 + lax.broadcasted_iota(jnp.int32, (tq, span), 0)) // CHUNK
    kc = (start + lax.broadcasted_iota(jnp.int32, (tq, span), 1)) // CHUNK
    bias = jnp.where(kc <= qc, jnp.where(kc >= qc - window // CHUNK, 0.0, NEG), NEG)
    _attend(q_ref, kw, vw, bias, sink_ref, g_ref, o_ref, acc_ref, n_kv=n_kv, hd=hd)


def _attn_prompt(zm3, zkv3, sink, g_out, window, n_kv, hd):
    bsz, s, _ = zm3.shape
    d_b = g_out.shape[0]
    kvw = n_kv * hd
    tq = _tile(s, ATTN_TQ)
    assert tq % CHUNK == 0 and window % CHUNK == 0 and tq + window <= s
    return pl.pallas_call(
        functools.partial(_attn_prompt_kernel, window=window, n_kv=n_kv, hd=hd),
        grid=(bsz, s // tq),
        in_specs=[pl.BlockSpec((None, tq, d_b), lambda b, t: (b, t, 2)),
                  pl.BlockSpec((None, s, kvw), lambda b, t: (b, 0, 0)),
                  pl.BlockSpec((None, s, kvw), lambda b, t: (b, 0, 1)),
                  pl.BlockSpec(memory_space=pltpu.SMEM),
                  pl.BlockSpec((1, d_b), lambda b, t: (0, 0))],
        out_specs=pl.BlockSpec((None, tq, d_b), lambda b, t: (b, t, 0)),
        out_shape=jax.ShapeDtypeStruct((bsz, s, d_b), BF16),
        scratch_shapes=[pltpu.VMEM((tq, d_b), F32)],
        compiler_params=_params(40, 2),
        name="attn_prompt",
    )(zm3, zkv3, zkv3, sink, g_out.reshape(1, d_b))


def _attn_sample_kernel(q_ref, kn_ref, vn_ref, ck_ref, cv_ref, sink_ref, g_ref, o_ref, acc_ref, *, n_kv, hd):
    kw = jnp.concatenate([ck_ref[...], kn_ref[...]], axis=0)
    vw = jnp.concatenate([cv_ref[...], vn_ref[...]], axis=0)
    _attend(q_ref, kw, vw, None, sink_ref, g_ref, o_ref, acc_ref, n_kv=n_kv, hd=hd)


def _attn_sample(zm3, zkv3, ck, cv, sink, g_out, n_kv, hd):
    bsz, t, _ = zm3.shape
    d_b = g_out.shape[0]
    kvw = n_kv * hd
    win = ck.shape[1]
    return pl.pallas_call(
        functools.partial(_attn_sample_kernel, n_kv=n_kv, hd=hd),
        grid=(bsz,),
        in_specs=[pl.BlockSpec((None, t, d_b), lambda b: (b, 0, 2)),
                  pl.BlockSpec((None, t, kvw), lambda b: (b, 0, 0)),
                  pl.BlockSpec((None, t, kvw), lambda b: (b, 0, 1)),
                  pl.BlockSpec((None, win, kvw), lambda b: (b, 0, 0)),
                  pl.BlockSpec((None, win, kvw), lambda b: (b, 0, 0)),
                  pl.BlockSpec(memory_space=pltpu.SMEM),
                  pl.BlockSpec((1, d_b), lambda b: (0, 0))],
        out_specs=pl.BlockSpec((None, t, d_b), lambda b: (b, 0, 0)),
        out_shape=jax.ShapeDtypeStruct((bsz, t, d_b), BF16),
        scratch_shapes=[pltpu.VMEM((t, d_b), F32)],
        compiler_params=_params(32, 1),
        name="attn_sample",
    )(zm3, zkv3, zkv3, ck.reshape(bsz, win, kvw), cv.reshape(bsz, win, kvw), sink, g_out.reshape(1, d_b))


def _conv_gate(a, b, prev, cw, cb):
    rows = a.shape[0]
    row = lax.broadcasted_iota(jnp.int32, a.shape, 0)
    a1 = jnp.where(row == 0, prev[1:2], pltpu.roll(a, 1, 0))
    a2 = jnp.where(row == 0, prev[0:1], jnp.where(row == 1, prev[1:2], pltpu.roll(a, 2, 0)))
    conv = cw[0:1] * a2 + cw[1:2] * a1 + cw[2:3] * a + cb
    return (jax.nn.silu(conv) * b).astype(BF16), a[rows - 2:rows]


def _up_prompt_kernel(h_ref, wa_ref, wb_ref, cw_ref, cb_ref, act_ref, nc_ref):
    tm = h_ref.shape[0]
    cw, cb = cw_ref[...], cb_ref[...]
    last = jnp.zeros((2, wa_ref.shape[1]), F32)
    rc = _tile(tm, UP_RC)
    for c in range(tm // rc):
        rs = slice(c * rc, (c + 1) * rc)
        h = h_ref[rs, :]
        a = jnp.dot(h, wa_ref[...], preferred_element_type=F32)
        b = jnp.dot(h, wb_ref[...], preferred_element_type=F32)
        act_ref[rs, :], last = _conv_gate(a, b, last, cw, cb)
    nc_ref[...] = last


def _up_proj_prompt(h, w_up, conv_w, conv_b, bsz, casts=()):
    m, d = h.shape
    d_ff = conv_w.shape[1]
    tm = m // bsz
    tn = _tile(d_ff, UP_TN)
    nj = d_ff // tn
    cast_in, cast_out, cast_shapes = _cast_specs(casts, bsz * nj, lambda i, j: i * nj + j)
    out = pl.pallas_call(
        _with_casts(_up_prompt_kernel, 5, 2, len(casts)),
        grid=(bsz, nj),
        in_specs=[pl.BlockSpec((tm, d), lambda i, j: (i, 0)),
                  pl.BlockSpec((d, tn), lambda i, j: (0, j)),
                  pl.BlockSpec((d, tn), lambda i, j: (0, nj + j)),
                  pl.BlockSpec((conv_w.shape[0], tn), lambda i, j: (0, j)),
                  pl.BlockSpec((1, tn), lambda i, j: (0, j))] + cast_in,
        out_specs=[pl.BlockSpec((tm, tn), lambda i, j: (i, j)),
                   pl.BlockSpec((None, 2, tn), lambda i, j: (i, 0, j))] + cast_out,
        out_shape=[jax.ShapeDtypeStruct((m, d_ff), BF16),
                   jax.ShapeDtypeStruct((bsz, 2, d_ff), F32)] + cast_shapes,
        compiler_params=_params(56, 2),
        name="up_proj_conv_gate",
    )(h, w_up, w_up, conv_w, conv_b.reshape(1, d_ff), *[c[0] for c in casts])
    return out[0], out[1], out[2:]


def _up_sample_kernel(h_ref, wa_ref, wb_ref, cw_ref, cb_ref, st_ref, act_ref, nc_ref):
    h = h_ref[...]
    a = jnp.dot(h, wa_ref[...], preferred_element_type=F32)
    b = jnp.dot(h, wb_ref[...], preferred_element_type=F32)
    cw, cb = cw_ref[...], cb_ref[...]
    pieces = st_ref.shape[0]
    t = h.shape[0] // pieces
    for p in range(pieces):
        rs = slice(p * t, (p + 1) * t)
        act_ref[rs, :], nc_ref[p] = _conv_gate(a[rs], b[rs], st_ref[p], cw, cb)


def _up_proj_sample(h, w_up, conv_w, conv_b, state):
    m, d = h.shape
    bsz = state.shape[0]
    d_ff = conv_w.shape[1]
    tn = _tile(d_ff, UP_TN)
    nj = d_ff // tn
    return pl.pallas_call(
        _up_sample_kernel,
        grid=(nj,),
        in_specs=[pl.BlockSpec((m, d), lambda j: (0, 0)),
                  pl.BlockSpec((d, tn), lambda j: (0, j)),
                  pl.BlockSpec((d, tn), lambda j: (0, nj + j)),
                  pl.BlockSpec((conv_w.shape[0], tn), lambda j: (0, j)),
                  pl.BlockSpec((1, tn), lambda j: (0, j)),
                  pl.BlockSpec((bsz, 2, tn), lambda j: (0, 0, j))],
        out_specs=[pl.BlockSpec((m, tn), lambda j: (0, j)),
                   pl.BlockSpec((bsz, 2, tn), lambda j: (0, 0, j))],
        out_shape=[jax.ShapeDtypeStruct((m, d_ff), BF16), jax.ShapeDtypeStruct((bsz, 2, d_ff), F32)],
        compiler_params=_params(32, 1),
        name="up_proj_conv_gate_sample",
    )(h, w_up, w_up, conv_w, conv_b.reshape(1, d_ff), state)


class _Weights:
    def __init__(self, depth):
        self.w_in = [None] * depth
        self.w_out = [None] * depth
        self.w_up = [None] * depth
        self.w_down = [None] * depth


def _trunk(x, mods, boff, p, wb, raw, cache, state):
    bsz, seq, d = x.shape
    depth = p["w_in"].shape[0]
    m = bsz * seq
    d_a, d_b = p["g_out_a"].shape[1], p["g_out_b"].shape[1]
    window, n_kv, hd = cache[0].shape[2:]
    kvw = n_kv * hd
    sample = raw is None
    n_blk = min(seq, p["sgu_w"].shape[2])
    ks, vs, convs, sgus = [], [], [], []
    h = _prenorm(x, p["g_pre_mix"][0], mods[0], boff, 1, 0)
    for l in range(depth):
        zm, zkv = _in_proj(h.reshape(m, d), wb.w_in[l], 2 * kvw)
        zm3, zkv3 = zm.reshape(bsz, seq, 3 * d_b), zkv.reshape(bsz, seq, 2 * kvw)
        k = zkv3[:, :, :kvw].reshape(bsz, seq, n_kv, hd)
        v = zkv3[:, :, kvw:].reshape(bsz, seq, n_kv, hd)
        if sample:
            a_n, vn = _sgu(zm, p["sgu_w"][l], p["sgu_b"][l], p["g_out_a"][l], n_blk, True)
            b_n = _attn_sample(zm3, zkv3, cache[0][l], cache[1][l], p["attn_sink"][l], p["g_out_b"][l], n_kv, hd)
            ks.append(k)
            vs.append(v)
            sgus.append(vn.reshape(bsz, seq, d_a))
        else:
            (a_n,) = _sgu(zm, p["sgu_w"][l], p["sgu_b"][l], p["g_out_a"][l], n_blk, False)
            b_n = _attn_prompt(zm3, zkv3, p["attn_sink"][l], p["g_out_b"][l], window, n_kv, hd)
            ks.append(k[:, -window:])
            vs.append(v[:, -window:])
        up_cast = ((raw[2], l),) if not sample and l == 0 else ()
        mix, done = _proj([a_n, b_n.reshape(m, d_b)], wb.w_out[l], OUT_TM,
                          OUT_TN_HOST if up_cast else OUT_TN, 48, "out_proj", up_cast)
        if up_cast:
            (wb.w_up[l],) = done
        x, h = _resid(x, mix, p["g_post_mix"][l], mods[l], boff, 2, (p["g_pre_ffn"][l], mods[l], 4, 3))
        if sample:
            act, new_conv = _up_proj_sample(h.reshape(m, d), wb.w_up[l], p["conv_w"][l], p["conv_b"][l], state[l])
        else:
            act, new_conv, (wb.w_down[l],) = _up_proj_prompt(
                h.reshape(m, d), wb.w_up[l], p["conv_w"][l], p["conv_b"][l], bsz, ((raw[3], l),))
        convs.append(new_conv)
        nxt_casts = () if sample or l + 1 == depth else tuple((w, l + 1) for w in raw[:3])
        f, done = _proj([act], wb.w_down[l], DOWN_TM, DOWN_TN, 56, "down_proj", nxt_casts)
        if nxt_casts:
            wb.w_in[l + 1], wb.w_out[l + 1], wb.w_up[l + 1] = done
        nxt = (p["g_pre_mix"][l + 1], mods[l + 1], 1, 0) if l + 1 < depth else None
        x, h = _resid(x, f, p["g_post_ffn"][l], mods[l], boff, 5, nxt)
    sgu = jnp.stack(sgus) if sample else None
    return x, jnp.stack(ks), jnp.stack(vs), jnp.stack(convs), sgu


def kernel(x_prompt, x_sample, c_prompt, c_sample, cache_k, cache_v, state_conv, w_mod, b_mod, g_pre_mix, g_post_mix, w_in, sgu_w, sgu_b, attn_sink, g_out_a, g_out_b, w_out, g_pre_ffn, g_post_ffn, w_up, conv_w, conv_b, w_down):
    depth, d = g_pre_mix.shape
    nb_p = c_prompt.shape[0]
    c_all = jnp.concatenate([c_prompt, c_sample], axis=0)
    mod = _modulation(c_all, w_mod, b_mod).reshape(depth, c_all.shape[0], N_MOD, 1, d)
    mods = [mod[l] for l in range(depth)]
    p = dict(g_pre_mix=g_pre_mix, g_post_mix=g_post_mix, w_in=w_in, sgu_w=sgu_w, sgu_b=sgu_b,
             attn_sink=attn_sink, g_out_a=g_out_a, g_out_b=g_out_b, g_pre_ffn=g_pre_ffn,
             g_post_ffn=g_post_ffn, conv_w=conv_w, conv_b=conv_b)
    wb = _Weights(depth)
    wb.w_in[0] = w_in[0].astype(BF16)
    wb.w_out[0] = w_out[0].astype(BF16)
    cache = (cache_k, cache_v)
    y_p, k_p, v_p, conv_p, _ = _trunk(x_prompt, mods, 0, p, wb, (w_in, w_out, w_up, w_down), cache, None)
    y_s, k_s, v_s, conv_s, sgu_s = _trunk(x_sample, mods, nb_p, p, wb, None, cache, state_conv)
    return (y_p, y_s, k_p, v_p, k_s, v_s, conv_p, conv_s, sgu_s)
```

```python
import functools

import jax
import jax.numpy as jnp
from jax import lax
from jax.experimental import pallas as pl
from jax.experimental.pallas import tpu as pltpu

F32 = jnp.float32
BF16 = jnp.bfloat16

CHUNK = 64
EPS = 1e-6
NEG = -1e30
N_MOD = 6
LOG2E = 1.4426950408889634
LANES = 128
BF16_ROWS = 16

MOD_TN = 512
NORM_TS = 256
PROJ_TM = 2048
OUT_TM, OUT_TN = 1024, 1024
OUT_TN_HOST = 512
UP_TN = 256
UP_RC = 256
DOWN_TM, DOWN_TN = 512, 512
SGU_ROWS = 512
ATTN_TQ = 128
MIB = 1024 * 1024


def _tile(full, pref):
    t = min(full, pref)
    assert full % t == 0, (full, pref)
    return t


def _params(vmem_mib, n_axes):
    return pltpu.CompilerParams(dimension_semantics=("arbitrary",) * n_axes,
                                vmem_limit_bytes=vmem_mib * MIB)


def _rms(x):
    return x * lax.rsqrt(jnp.mean(x * x, axis=-1, keepdims=True) + EPS)


def _with_casts(body, n_in, n_out, n_cast):
    if not n_cast:
        return body

    def kernel(*refs):
        ins = refs[:n_in]
        cast_in = refs[n_in:n_in + n_cast]
        outs = refs[n_in + n_cast:n_in + n_cast + n_out]
        cast_out = refs[n_in + n_cast + n_out:n_in + 2 * n_cast + n_out]
        for src, dst in zip(cast_in, cast_out):
            dst[...] = src[...].astype(BF16)
        body(*ins, *outs, *refs[n_in + 2 * n_cast + n_out:])

    return kernel


def _cast_specs(casts, n_steps, step_of):
    in_specs, out_specs, shapes = [], [], []
    for w, l in casts:
        _, rows, cols = w.shape
        slab = rows // n_steps
        assert slab * n_steps == rows and slab % BF16_ROWS == 0, (w.shape, n_steps)
        in_specs.append(pl.BlockSpec((None, slab, cols), lambda *g, l=l: (l, step_of(*g), 0)))
        out_specs.append(pl.BlockSpec((slab, cols), lambda *g: (step_of(*g), 0)))
        shapes.append(jax.ShapeDtypeStruct((rows, cols), BF16))
    return in_specs, out_specs, shapes


def _mod_kernel(c_ref, w_ref, b_ref, o_ref):
    act = jax.nn.silu(c_ref[...]).astype(BF16)
    o_ref[...] = jnp.dot(act, w_ref[...].astype(BF16), preferred_element_type=F32) + b_ref[...]


def _modulation(c, w_mod, b_mod):
    depth, d, n = w_mod.shape
    rows = c.shape[0]
    tn = _tile(n, MOD_TN)
    return pl.pallas_call(
        _mod_kernel,
        grid=(depth, n // tn),
        in_specs=[pl.BlockSpec((rows, d), lambda l, j: (0, 0)),
                  pl.BlockSpec((None, d, tn), lambda l, j: (l, 0, j)),
                  pl.BlockSpec((None, 1, tn), lambda l, j: (l, 0, j))],
        out_specs=pl.BlockSpec((None, rows, tn), lambda l, j: (l, 0, j)),
        out_shape=jax.ShapeDtypeStruct((depth, rows, n), F32),
        compiler_params=_params(40, 2),
        name="modulation",
    )(c, w_mod, b_mod.reshape(depth, 1, n))


def _prenorm_kernel(x_ref, g_ref, sc_ref, sh_ref, h_ref):
    y = _rms(x_ref[...]) * g_ref[...]
    h_ref[...] = (y * (1 + sc_ref[...]) + sh_ref[...]).astype(BF16)


def _mod_spec(boff, j, d):
    return pl.BlockSpec((None, None, 1, d), lambda b, t: (boff + b, j, 0, 0))


def _prenorm(x, g, mod, boff, j_sc, j_sh):
    bsz, s, d = x.shape
    ts = _tile(s, NORM_TS)
    return pl.pallas_call(
        _prenorm_kernel,
        grid=(bsz, s // ts),
        in_specs=[pl.BlockSpec((None, ts, d), lambda b, t: (b, t, 0)),
                  pl.BlockSpec((1, d), lambda b, t: (0, 0)),
                  _mod_spec(boff, j_sc, d), _mod_spec(boff, j_sh, d)],
        out_specs=pl.BlockSpec((None, ts, d), lambda b, t: (b, t, 0)),
        out_shape=jax.ShapeDtypeStruct((bsz, s, d), BF16),
        compiler_params=_params(32, 2),
        name="prenorm",
    )(x, g.reshape(1, d), mod, mod)


def _resid_kernel(x_ref, m_ref, gt_ref, gpost_ref, *rest, with_next):
    xn = x_ref[...] + gt_ref[...] * (_rms(m_ref[...].astype(F32)) * gpost_ref[...])
    if with_next:
        gpre_ref, sc_ref, sh_ref, xo_ref, h_ref = rest
        y = _rms(xn) * gpre_ref[...]
        h_ref[...] = (y * (1 + sc_ref[...]) + sh_ref[...]).astype(BF16)
    else:
        (xo_ref,) = rest
    xo_ref[...] = xn


def _resid(x, m, g_post, mod, boff, j_gt, nxt=None):
    bsz, s, d = x.shape
    ts = _tile(s, NORM_TS)
    row = pl.BlockSpec((None, ts, d), lambda b, t: (b, t, 0))
    vec = pl.BlockSpec((1, d), lambda b, t: (0, 0))
    in_specs = [row, row, _mod_spec(boff, j_gt, d), vec]
    args = [x, m.reshape(bsz, s, d), mod, g_post.reshape(1, d)]
    out_specs = [row]
    out_shape = [jax.ShapeDtypeStruct((bsz, s, d), F32)]
    if nxt is not None:
        g_pre, mod_next, j_sc, j_sh = nxt
        in_specs += [vec, _mod_spec(boff, j_sc, d), _mod_spec(boff, j_sh, d)]
        args += [g_pre.reshape(1, d), mod_next, mod_next]
        out_specs.append(row)
        out_shape.append(jax.ShapeDtypeStruct((bsz, s, d), BF16))
    out = pl.pallas_call(
        functools.partial(_resid_kernel, with_next=nxt is not None),
        grid=(bsz, s // ts),
        in_specs=in_specs, out_specs=out_specs, out_shape=out_shape,
        compiler_params=_params(48, 2),
        name="resid_norm",
    )(*args)
    return out if nxt is not None else (out[0], None)


def _in_proj_kernel(a_ref, w_ref, zm_ref, zkv_ref):
    r = jnp.dot(a_ref[...], w_ref[...], preferred_element_type=F32)
    zm_ref[...] = r.astype(BF16)

    @pl.when(pl.program_id(1) == 0)
    def _():
        zkv_ref[...] = r


def _in_proj(a, w, kv_cols):
    m, k = a.shape
    n = w.shape[1]
    tm, tn = _tile(m, PROJ_TM), kv_cols
    assert (n - kv_cols) % tn == 0
    n_main = (n - kv_cols) // tn
    return pl.pallas_call(
        _in_proj_kernel,
        grid=(m // tm, n_main + 1),
        in_specs=[pl.BlockSpec((tm, k), lambda i, j: (i, 0)),
                  pl.BlockSpec((k, tn), lambda i, j: (0, jnp.where(j == 0, n_main, j - 1)))],
        out_specs=[pl.BlockSpec((tm, tn), lambda i, j: (i, jnp.maximum(j - 1, 0))),
                   pl.BlockSpec((tm, tn), lambda i, j: (i, 0))],
        out_shape=[jax.ShapeDtypeStruct((m, n - kv_cols), BF16), jax.ShapeDtypeStruct((m, kv_cols), F32)],
        compiler_params=_params(56, 2),
        name="in_proj",
    )(a, w)


def _mm_kernel(a_ref, w_ref, o_ref):
    o_ref[...] = jnp.dot(a_ref[...], w_ref[...], preferred_element_type=F32).astype(o_ref.dtype)


def _mm2_kernel(a1_ref, a2_ref, w1_ref, w2_ref, o_ref):
    o_ref[...] = (jnp.dot(a1_ref[...], w1_ref[...], preferred_element_type=F32)
                  + jnp.dot(a2_ref[...], w2_ref[...], preferred_element_type=F32)).astype(o_ref.dtype)


def _proj(acts, w, tm, tn, vmem_mib, name, casts=()):
    m, k = acts[0].shape
    assert all(a.shape == (m, k) for a in acts) and w.shape[0] == k * len(acts)
    n = w.shape[1]
    tm, tn = _tile(m, tm), _tile(n, tn)
    nj = n // tn
    n_steps = (m // tm) * nj
    cast_in, cast_out, cast_shapes = _cast_specs(casts, n_steps, lambda i, j: i * nj + j)
    body = _mm_kernel if len(acts) == 1 else _mm2_kernel
    out = pl.pallas_call(
        _with_casts(body, 2 * len(acts), 1, len(casts)),
        grid=(m // tm, nj),
        in_specs=([pl.BlockSpec((tm, k), lambda i, j: (i, 0))] * len(acts)
                  + [pl.BlockSpec((k, tn), lambda i, j, r=r: (r, j)) for r in range(len(acts))]
                  + cast_in),
        out_specs=[pl.BlockSpec((tm, tn), lambda i, j: (i, j))] + cast_out,
        out_shape=[jax.ShapeDtypeStruct((m, n), BF16)] + cast_shapes,
        compiler_params=_params(vmem_mib, 2),
        name=name,
    )(*acts, *([w] * len(acts)), *[c[0] for c in casts])
    return out[0], out[1:]


def _sgu_kernel(u_ref, v_ref, w_ref, bt_ref, g_ref, o_ref, *rest, n, groups, with_vn):
    if with_vn:
        vn_ref, acc_ref = rest
    else:
        (acc_ref,) = rest
    rows, d_a = u_ref.shape
    gd = d_a // groups
    qi = lax.broadcasted_iota(jnp.int32, (n, n), 0) // CHUNK
    kj = lax.broadcasted_iota(jnp.int32, (n, n), 1) // CHUNK
    causal = kj <= qi
    sq = [jnp.zeros((n, gd), F32) for _ in range(rows // n)]
    for g in range(groups):
        wg = jnp.where(causal, w_ref[g], 0.0).astype(BF16)
        bias = bt_ref[:, g:g + 1]
        cols = slice(g * gd, (g + 1) * gd)
        for sb in range(rows // n):
            rs = slice(sb * n, (sb + 1) * n)
            vn = _rms(v_ref[rs, cols].astype(F32))
            if with_vn:
                vn_ref[rs, cols] = vn
            s = jnp.dot(wg, vn.astype(BF16), preferred_element_type=F32) + bias
            out = u_ref[rs, cols].astype(F32) * s
            acc_ref[rs, cols] = out
            sq[sb] = sq[sb] + out * out
    for sb in range(rows // n):
        rs = slice(sb * n, (sb + 1) * n)
        inv = lax.rsqrt(jnp.sum(sq[sb], axis=-1, keepdims=True) / d_a + EPS)
        o_ref[rs, :] = (acc_ref[rs, :] * inv * g_ref[...]).astype(BF16)


def _sgu(z, w_s, b_s, g_out, n, with_vn):
    m = z.shape[0]
    d_a = g_out.shape[0]
    groups = w_s.shape[0]
    rows = _tile(m, max(n, SGU_ROWS))
    out_specs = [pl.BlockSpec((rows, d_a), lambda i: (i, 0))]
    out_shape = [jax.ShapeDtypeStruct((m, d_a), BF16)]
    if with_vn:
        out_specs.append(pl.BlockSpec((rows, d_a), lambda i: (i, 0)))
        out_shape.append(jax.ShapeDtypeStruct((m, d_a), F32))
    return pl.pallas_call(
        functools.partial(_sgu_kernel, n=n, groups=groups, with_vn=with_vn),
        grid=(m // rows,),
        in_specs=[pl.BlockSpec((rows, d_a), lambda i: (i, 0)),
                  pl.BlockSpec((rows, d_a), lambda i: (i, 1)),
                  pl.BlockSpec((groups, n, n), lambda i: (0, 0, 0)),
                  pl.BlockSpec((n, groups), lambda i: (0, 0)),
                  pl.BlockSpec((1, d_a), lambda i: (0, 0))],
        out_specs=out_specs, out_shape=out_shape,
        scratch_shapes=[pltpu.VMEM((rows, d_a), F32)],
        compiler_params=_params(40, 1),
        name="sgu_mixer",
    )(z, z, w_s[:, :n, :n], b_s[:, :n].T, g_out.reshape(1, d_a))


def _attend(q_ref, kw, vw, bias, sink_ref, g_ref, o_ref, acc_ref, *, n_kv, hd):
    tq, d_b = q_ref.shape
    span = kw.shape[0]
    rep = d_b // (n_kv * hd)
    assert 2 * hd == LANES and rep % 2 == 0
    kscale = hd ** -0.5 * LOG2E
    zeros = jnp.zeros((span, hd), BF16)
    ones = jnp.ones((span, hd), BF16)
    first = lax.broadcasted_iota(jnp.int32, (tq, LANES), 1) < hd
    nt = (((1,), (1,)), ((), ()))
    for g in range(n_kv):
        kg = (kw[:, g * hd:(g + 1) * hd] * kscale).astype(BF16)
        vg = vw[:, g * hd:(g + 1) * hd].astype(BF16)
        keys = (jnp.concatenate([kg, zeros], axis=1), jnp.concatenate([zeros, kg], axis=1))
        vals = (jnp.concatenate([vg, zeros, ones, zeros], axis=1),
                jnp.concatenate([zeros, vg, zeros, ones], axis=1))
        for p in range(rep // 2):
            h0 = g * rep + 2 * p
            cols = slice(h0 * hd, (h0 + 2) * hd)
            qp = q_ref[:, cols]
            out, sink_e = None, []
            for i in range(2):
                s = lax.dot_general(qp, keys[i], nt, preferred_element_type=F32)
                if bias is not None:
                    s = s + bias
                sk = sink_ref[h0 + i] * LOG2E
                mx = jnp.maximum(jnp.max(s, axis=-1, keepdims=True), sk)
                e = jnp.exp2(s - mx).astype(BF16)
                sink_e.append(jnp.exp2(sk - mx))
                o = jnp.dot(e, vals[i], preferred_element_type=F32)
                out = o if out is None else out + o
            den = out[:, LANES:] + jnp.where(first, sink_e[0], sink_e[1])
            acc_ref[:, cols] = out[:, :LANES] / den
    o_ref[...] = (_rms(acc_ref[...]) * g_ref[...]).astype(BF16)


def _attn_prompt_kernel(q_ref, k_ref, v_ref, sink_ref, g_ref, o_ref, acc_ref, *, window, n_kv, hd):
    tq = q_ref.shape[0]
    span = tq + window
    q0 = pl.program_id(1) * tq
    start = pl.multiple_of(jnp.maximum(q0 - window, 0), CHUNK)
    kw = k_ref[pl.ds(start, span), :]
    vw = v_ref[pl.ds(start, span), :]
    qc = (q0 + lax.broadcasted_iota(jnp.int32, (tq, span), 0)) // CHUNK
    kc = (start + lax.broadcasted_iota(jnp.int32, (tq, span), 1)) // CHUNK
    bias = jnp.where(kc <= qc, jnp.where(kc >= qc - window // CHUNK, 0.0, NEG), NEG)
    _attend(q_ref, kw, vw, bias, sink_ref, g_ref, o_ref, acc_ref, n_kv=n_kv, hd=hd)


def _attn_prompt(zm3, zkv3, sink, g_out, window, n_kv, hd):
    bsz, s, _ = zm3.shape
    d_b = g_out.shape[0]
    kvw = n_kv * hd
    tq = _tile(s, ATTN_TQ)
    assert tq % CHUNK == 0 and window % CHUNK == 0 and tq + window <= s
    return pl.pallas_call(
        functools.partial(_attn_prompt_kernel, window=window, n_kv=n_kv, hd=hd),
        grid=(bsz, s // tq),
        in_specs=[pl.BlockSpec((None, tq, d_b), lambda b, t: (b, t, 2)),
                  pl.BlockSpec((None, s, kvw), lambda b, t: (b, 0, 0)),
                  pl.BlockSpec((None, s, kvw), lambda b, t: (b, 0, 1)),
                  pl.BlockSpec(memory_space=pltpu.SMEM),
                  pl.BlockSpec((1, d_b), lambda b, t: (0, 0))],
        out_specs=pl.BlockSpec((None, tq, d_b), lambda b, t: (b, t, 0)),
        out_shape=jax.ShapeDtypeStruct((bsz, s, d_b), BF16),
        scratch_shapes=[pltpu.VMEM((tq, d_b), F32)],
        compiler_params=_params(40, 2),
        name="attn_prompt",
    )(zm3, zkv3, zkv3, sink, g_out.reshape(1, d_b))


def _attn_sample_kernel(q_ref, kn_ref, vn_ref, ck_ref, cv_ref, sink_ref, g_ref, o_ref, acc_ref, *, n_kv, hd):
    kw = jnp.concatenate([ck_ref[...], kn_ref[...]], axis=0)
    vw = jnp.concatenate([cv_ref[...], vn_ref[...]], axis=0)
    _attend(q_ref, kw, vw, None, sink_ref, g_ref, o_ref, acc_ref, n_kv=n_kv, hd=hd)


def _attn_sample(zm3, zkv3, ck, cv, sink, g_out, n_kv, hd):
    bsz, t, _ = zm3.shape
    d_b = g_out.shape[0]
    kvw = n_kv * hd
    win = ck.shape[1]
    return pl.pallas_call(
        functools.partial(_attn_sample_kernel, n_kv=n_kv, hd=hd),
        grid=(bsz,),
        in_specs=[pl.BlockSpec((None, t, d_b), lambda b: (b, 0, 2)),
                  pl.BlockSpec((None, t, kvw), lambda b: (b, 0, 0)),
                  pl.BlockSpec((None, t, kvw), lambda b: (b, 0, 1)),
                  pl.BlockSpec((None, win, kvw), lambda b: (b, 0, 0)),
                  pl.BlockSpec((None, win, kvw), lambda b: (b, 0, 0)),
                  pl.BlockSpec(memory_space=pltpu.SMEM),
                  pl.BlockSpec((1, d_b), lambda b: (0, 0))],
        out_specs=pl.BlockSpec((None, t, d_b), lambda b: (b, 0, 0)),
        out_shape=jax.ShapeDtypeStruct((bsz, t, d_b), BF16),
        scratch_shapes=[pltpu.VMEM((t, d_b), F32)],
        compiler_params=_params(32, 1),
        name="attn_sample",
    )(zm3, zkv3, zkv3, ck.reshape(bsz, win, kvw), cv.reshape(bsz, win, kvw), sink, g_out.reshape(1, d_b))


def _conv_gate(a, b, prev, cw, cb):
    rows = a.shape[0]
    row = lax.broadcasted_iota(jnp.int32, a.shape, 0)
    a1 = jnp.where(row == 0, prev[1:2], pltpu.roll(a, 1, 0))
    a2 = jnp.where(row == 0, prev[0:1], jnp.where(row == 1, prev[1:2], pltpu.roll(a, 2, 0)))
    conv = cw[0:1] * a2 + cw[1:2] * a1 + cw[2:3] * a + cb
    return (jax.nn.silu(conv) * b).astype(BF16), a[rows - 2:rows]


def _up_prompt_kernel(h_ref, wa_ref, wb_ref, cw_ref, cb_ref, act_ref, nc_ref):
    tm = h_ref.shape[0]
    tn = wa_ref.shape[1]
    cols = pl.ds(pl.multiple_of(pl.program_id(1) * tn, tn), tn)
    cw, cb = cw_ref[:, cols], cb_ref[:, cols]
    last = jnp.zeros((2, tn), F32)
    rc = _tile(tm, UP_RC)
    for c in range(tm // rc):
        rs = slice(c * rc, (c + 1) * rc)
        h = h_ref[rs, :]
        a = jnp.dot(h, wa_ref[...], preferred_element_type=F32)
        b = jnp.dot(h, wb_ref[...], preferred_element_type=F32)
        act_ref[rs, :], last = _conv_gate(a, b, last, cw, cb)
    nc_ref[:, cols] = last


def _up_proj_prompt(h, w_up, conv_w, conv_b, bsz, casts=()):
    m, d = h.shape
    d_ff = conv_w.shape[1]
    tm = m // bsz
    tn = _tile(d_ff, UP_TN)
    nj = d_ff // tn
    cast_in, cast_out, cast_shapes = _cast_specs(casts, bsz * nj, lambda i, j: i * nj + j)
    out = pl.pallas_call(
        _with_casts(_up_prompt_kernel, 5, 2, len(casts)),
        grid=(bsz, nj),
        in_specs=[pl.BlockSpec((tm, d), lambda i, j: (i, 0)),
                  pl.BlockSpec((d, tn), lambda i, j: (0, j)),
                  pl.BlockSpec((d, tn), lambda i, j: (0, nj + j)),
                  pl.BlockSpec((conv_w.shape[0], d_ff), lambda i, j: (0, 0)),
                  pl.BlockSpec((1, d_ff), lambda i, j: (0, 0))] + cast_in,
        out_specs=[pl.BlockSpec((tm, tn), lambda i, j: (i, j)),
                   pl.BlockSpec((None, 2, d_ff), lambda i, j: (i, 0, 0))] + cast_out,
        out_shape=[jax.ShapeDtypeStruct((m, d_ff), BF16),
                   jax.ShapeDtypeStruct((bsz, 2, d_ff), F32)] + cast_shapes,
        compiler_params=_params(56, 2),
        name="up_proj_conv_gate",
    )(h, w_up, w_up, conv_w, conv_b.reshape(1, d_ff), *[c[0] for c in casts])
    return out[0], out[1], out[2:]


def _up_sample_kernel(h_ref, wa_ref, wb_ref, cw_ref, cb_ref, st_ref, act_ref, nc_ref):
    h = h_ref[...]
    a = jnp.dot(h, wa_ref[...], preferred_element_type=F32)
    b = jnp.dot(h, wb_ref[...], preferred_element_type=F32)
    cw, cb = cw_ref[...], cb_ref[...]
    pieces = st_ref.shape[0]
    t = h.shape[0] // pieces
    for p in range(pieces):
        rs = slice(p * t, (p + 1) * t)
        act_ref[rs, :], nc_ref[p] = _conv_gate(a[rs], b[rs], st_ref[p], cw, cb)


def _up_proj_sample(h, w_up, conv_w, conv_b, state):
    m, d = h.shape
    bsz = state.shape[0]
    d_ff = conv_w.shape[1]
    tn = _tile(d_ff, UP_TN)
    nj = d_ff // tn
    return pl.pallas_call(
        _up_sample_kernel,
        grid=(nj,),
        in_specs=[pl.BlockSpec((m, d), lambda j: (0, 0)),
                  pl.BlockSpec((d, tn), lambda j: (0, j)),
                  pl.BlockSpec((d, tn), lambda j: (0, nj + j)),
                  pl.BlockSpec((conv_w.shape[0], tn), lambda j: (0, j)),
                  pl.BlockSpec((1, tn), lambda j: (0, j)),
                  pl.BlockSpec((bsz, 2, tn), lambda j: (0, 0, j))],
        out_specs=[pl.BlockSpec((m, tn), lambda j: (0, j)),
                   pl.BlockSpec((bsz, 2, tn), lambda j: (0, 0, j))],
        out_shape=[jax.ShapeDtypeStruct((m, d_ff), BF16), jax.ShapeDtypeStruct((bsz, 2, d_ff), F32)],
        compiler_params=_params(32, 1),
        name="up_proj_conv_gate_sample",
    )(h, w_up, w_up, conv_w, conv_b.reshape(1, d_ff), state)


class _Weights:
    def __init__(self, depth):
        self.w_in = [None] * depth
        self.w_out = [None] * depth
        self.w_up = [None] * depth
        self.w_down = [None] * depth


def _trunk(x, mods, boff, p, wb, raw, cache, state):
    bsz, seq, d = x.shape
    depth = p["w_in"].shape[0]
    m = bsz * seq
    d_a, d_b = p["g_out_a"].shape[1], p["g_out_b"].shape[1]
    window, n_kv, hd = cache[0].shape[2:]
    kvw = n_kv * hd
    sample = raw is None
    n_blk = min(seq, p["sgu_w"].shape[2])
    ks, vs, convs, sgus = [], [], [], []
    h = _prenorm(x, p["g_pre_mix"][0], mods[0], boff, 1, 0)
    for l in range(depth):
        zm, zkv = _in_proj(h.reshape(m, d), wb.w_in[l], 2 * kvw)
        zm3, zkv3 = zm.reshape(bsz, seq, 3 * d_b), zkv.reshape(bsz, seq, 2 * kvw)
        kept = seq if sample else window
        tail = zkv3[:, seq - kept:, :]
        ks.append(tail[:, :, :kvw].reshape(bsz, kept, n_kv, hd))
        vs.append(tail[:, :, kvw:].reshape(bsz, kept, n_kv, hd))
        if sample:
            a_n, vn = _sgu(zm, p["sgu_w"][l], p["sgu_b"][l], p["g_out_a"][l], n_blk, True)
            b_n = _attn_sample(zm3, zkv3, cache[0][l], cache[1][l], p["attn_sink"][l], p["g_out_b"][l], n_kv, hd)
            sgus.append(vn.reshape(bsz, seq, d_a))
        else:
            (a_n,) = _sgu(zm, p["sgu_w"][l], p["sgu_b"][l], p["g_out_a"][l], n_blk, False)
            b_n = _attn_prompt(zm3, zkv3, p["attn_sink"][l], p["g_out_b"][l], window, n_kv, hd)
        up_cast = ((raw[2], l),) if not sample and l == 0 else ()
        mix, done = _proj([a_n, b_n.reshape(m, d_b)], wb.w_out[l], OUT_TM,
                          OUT_TN_HOST if up_cast else OUT_TN, 48, "out_proj", up_cast)
        if up_cast:
            (wb.w_up[l],) = done
        x, h = _resid(x, mix, p["g_post_mix"][l], mods[l], boff, 2, (p["g_pre_ffn"][l], mods[l], 4, 3))
        if sample:
            act, new_conv = _up_proj_sample(h.reshape(m, d), wb.w_up[l], p["conv_w"][l], p["conv_b"][l], state[l])
        else:
            act, new_conv, (wb.w_down[l],) = _up_proj_prompt(
                h.reshape(m, d), wb.w_up[l], p["conv_w"][l], p["conv_b"][l], bsz, ((raw[3], l),))
        convs.append(new_conv)
        nxt_casts = () if sample or l + 1 == depth else tuple((w, l + 1) for w in raw[:3])
        f, done = _proj([act], wb.w_down[l], DOWN_TM, DOWN_TN, 56, "down_proj", nxt_casts)
        if nxt_casts:
            wb.w_in[l + 1], wb.w_out[l + 1], wb.w_up[l + 1] = done
        nxt = (p["g_pre_mix"][l + 1], mods[l + 1], 1, 0) if l + 1 < depth else None
        x, h = _resid(x, f, p["g_post_ffn"][l], mods[l], boff, 5, nxt)
    sgu = jnp.stack(sgus) if sample else None
    return x, jnp.stack(ks), jnp.stack(vs), jnp.stack(convs), sgu


def kernel(x_prompt, x_sample, c_prompt, c_sample, cache_k, cache_v, state_conv, w_mod, b_mod, g_pre_mix, g_post_mix, w_in, sgu_w, sgu_b, attn_sink, g_out_a, g_out_b, w_out, g_pre_ffn, g_post_ffn, w_up, conv_w, conv_b, w_down):
    depth, d = g_pre_mix.shape
    nb_p = c_prompt.shape[0]
    c_all = jnp.concatenate([c_prompt, c_sample], axis=0)
    mod = _modulation(c_all, w_mod, b_mod).reshape(depth, c_all.shape[0], N_MOD, 1, d)
    mods = [mod[l] for l in range(depth)]
    p = dict(g_pre_mix=g_pre_mix, g_post_mix=g_post_mix, w_in=w_in, sgu_w=sgu_w, sgu_b=sgu_b,
             attn_sink=attn_sink, g_out_a=g_out_a, g_out_b=g_out_b, g_pre_ffn=g_pre_ffn,
             g_post_ffn=g_post_ffn, conv_w=conv_w, conv_b=conv_b)
    wb = _Weights(depth)
    wb.w_in[0] = w_in[0].astype(BF16)
    wb.w_out[0] = w_out[0].astype(BF16)
    cache = (cache_k, cache_v)
    y_p, k_p, v_p, conv_p, _ = _trunk(x_prompt, mods, 0, p, wb, (w_in, w_out, w_up, w_down), cache, None)
    y_s, k_s, v_s, conv_s, sgu_s = _trunk(x_sample, mods, nb_p, p, wb, None, cache, state_conv)
    return (y_p, y_s, k_p, v_p, k_s, v_s, conv_p, conv_s, sgu_s)
```

```python
import functools

import jax
import jax.numpy as jnp
from jax import lax
from jax.experimental import pallas as pl
from jax.experimental.pallas import tpu as pltpu

F32 = jnp.float32
BF16 = jnp.bfloat16

CHUNK = 64
EPS = 1e-6
NEG = -1e30
N_MOD = 6
LOG2E = 1.4426950408889634
LANES = 128
BF16_ROWS = 16

MOD_TN = 512
NORM_TS = 512
RESID_TS = 256
PROJ_TM = 2048
OUT_TM, OUT_TN = 1024, 1024
OUT_TN_HOST = 512
UP_TN = 256
UP_RC = 256
DOWN_TM, DOWN_TN = 512, 512
SGU_ROWS = 512
ATTN_TQ = 128
MIB = 1024 * 1024


def _tile(full, pref):
    t = min(full, pref)
    assert full % t == 0, (full, pref)
    return t


def _params(vmem_mib, n_axes):
    return pltpu.CompilerParams(dimension_semantics=("arbitrary",) * n_axes,
                                vmem_limit_bytes=vmem_mib * MIB)


def _rms(x):
    return x * lax.rsqrt(jnp.mean(x * x, axis=-1, keepdims=True) + EPS)


def _with_casts(body, n_in, n_out, n_cast):
    if not n_cast:
        return body

    def kernel(*refs):
        ins = refs[:n_in]
        cast_in = refs[n_in:n_in + n_cast]
        outs = refs[n_in + n_cast:n_in + n_cast + n_out]
        cast_out = refs[n_in + n_cast + n_out:n_in + 2 * n_cast + n_out]
        for src, dst in zip(cast_in, cast_out):
            dst[...] = src[...].astype(BF16)
        body(*ins, *outs, *refs[n_in + 2 * n_cast + n_out:])

    return kernel


def _cast_specs(casts, n_steps, step_of):
    in_specs, out_specs, shapes = [], [], []
    for w, l, cb in casts:
        _, rows, cols = w.shape
        assert n_steps % cb == 0 and cols % cb == 0
        slab, width = rows // (n_steps // cb), cols // cb
        assert slab * n_steps == rows * cb and slab % BF16_ROWS == 0 and (width % LANES == 0 or cb == 1), (
            w.shape, n_steps, cb)
        in_specs.append(pl.BlockSpec((None, slab, width),
                                     lambda *g, l=l, cb=cb: (l, step_of(*g) // cb, step_of(*g) % cb)))
        out_specs.append(pl.BlockSpec((slab, width), lambda *g, cb=cb: (step_of(*g) // cb, step_of(*g) % cb)))
        shapes.append(jax.ShapeDtypeStruct((rows, cols), BF16))
    return in_specs, out_specs, shapes


def _mod_kernel(c_ref, w_ref, b_ref, o_ref):
    act = jax.nn.silu(c_ref[...]).astype(BF16)
    o_ref[...] = jnp.dot(act, w_ref[...].astype(BF16), preferred_element_type=F32) + b_ref[...]


def _modulation(c, w_mod, b_mod):
    depth, d, n = w_mod.shape
    rows = c.shape[0]
    tn = _tile(n, MOD_TN)
    return pl.pallas_call(
        _mod_kernel,
        grid=(depth, n // tn),
        in_specs=[pl.BlockSpec((rows, d), lambda l, j: (0, 0)),
                  pl.BlockSpec((None, d, tn), lambda l, j: (l, 0, j)),
                  pl.BlockSpec((None, 1, tn), lambda l, j: (l, 0, j))],
        out_specs=pl.BlockSpec((None, rows, tn), lambda l, j: (l, 0, j)),
        out_shape=jax.ShapeDtypeStruct((depth, rows, n), F32),
        compiler_params=_params(40, 2),
        name="modulation",
    )(c, w_mod, b_mod.reshape(depth, 1, n))


def _prenorm_kernel(x_ref, g_ref, sc_ref, sh_ref, h_ref):
    y = _rms(x_ref[...]) * g_ref[...]
    h_ref[...] = (y * (1 + sc_ref[...]) + sh_ref[...]).astype(BF16)


def _mod_spec(boff, j, d):
    return pl.BlockSpec((None, None, 1, d), lambda b, t: (boff + b, j, 0, 0))


def _prenorm(x, g, mod, boff, j_sc, j_sh):
    bsz, s, d = x.shape
    ts = _tile(s, NORM_TS)
    return pl.pallas_call(
        _prenorm_kernel,
        grid=(bsz, s // ts),
        in_specs=[pl.BlockSpec((None, ts, d), lambda b, t: (b, t, 0)),
                  pl.BlockSpec((1, d), lambda b, t: (0, 0)),
                  _mod_spec(boff, j_sc, d), _mod_spec(boff, j_sh, d)],
        out_specs=pl.BlockSpec((None, ts, d), lambda b, t: (b, t, 0)),
        out_shape=jax.ShapeDtypeStruct((bsz, s, d), BF16),
        compiler_params=_params(32, 2),
        name="prenorm",
    )(x, g.reshape(1, d), mod, mod)


def _resid_kernel(x_ref, m_ref, gt_ref, gpost_ref, *rest, with_next):
    xn = x_ref[...] + gt_ref[...] * (_rms(m_ref[...].astype(F32)) * gpost_ref[...])
    if with_next:
        gpre_ref, sc_ref, sh_ref, xo_ref, h_ref = rest
        y = _rms(xn) * gpre_ref[...]
        h_ref[...] = (y * (1 + sc_ref[...]) + sh_ref[...]).astype(BF16)
    else:
        (xo_ref,) = rest
    xo_ref[...] = xn


def _resid(x, m, g_post, mod, boff, j_gt, nxt=None):
    bsz, s, d = x.shape
    ts = _tile(s, RESID_TS)
    row = pl.BlockSpec((None, ts, d), lambda b, t: (b, t, 0))
    vec = pl.BlockSpec((1, d), lambda b, t: (0, 0))
    in_specs = [row, row, _mod_spec(boff, j_gt, d), vec]
    args = [x, m.reshape(bsz, s, d), mod, g_post.reshape(1, d)]
    out_specs = [row]
    out_shape = [jax.ShapeDtypeStruct((bsz, s, d), F32)]
    if nxt is not None:
        g_pre, mod_next, j_sc, j_sh = nxt
        in_specs += [vec, _mod_spec(boff, j_sc, d), _mod_spec(boff, j_sh, d)]
        args += [g_pre.reshape(1, d), mod_next, mod_next]
        out_specs.append(row)
        out_shape.append(jax.ShapeDtypeStruct((bsz, s, d), BF16))
    out = pl.pallas_call(
        functools.partial(_resid_kernel, with_next=nxt is not None),
        grid=(bsz, s // ts),
        in_specs=in_specs, out_specs=out_specs, out_shape=out_shape,
        compiler_params=_params(48, 2),
        name="resid_norm",
    )(*args)
    return out if nxt is not None else (out[0], None)


def _in_proj_kernel(a_ref, w_ref, zm_ref, zkv_ref):
    r = jnp.dot(a_ref[...], w_ref[...], preferred_element_type=F32)
    zm_ref[...] = r.astype(BF16)

    @pl.when(pl.program_id(1) == 0)
    def _():
        zkv_ref[...] = r


def _in_proj(a, w, kv_cols):
    m, k = a.shape
    n = w.shape[1]
    tm, tn = _tile(m, PROJ_TM), kv_cols
    assert (n - kv_cols) % tn == 0
    n_main = (n - kv_cols) // tn
    return pl.pallas_call(
        _in_proj_kernel,
        grid=(m // tm, n_main + 1),
        in_specs=[pl.BlockSpec((tm, k), lambda i, j: (i, 0)),
                  pl.BlockSpec((k, tn), lambda i, j: (0, jnp.where(j == 0, n_main, j - 1)))],
        out_specs=[pl.BlockSpec((tm, tn), lambda i, j: (i, jnp.maximum(j - 1, 0))),
                   pl.BlockSpec((tm, tn), lambda i, j: (i, 0))],
        out_shape=[jax.ShapeDtypeStruct((m, n - kv_cols), BF16), jax.ShapeDtypeStruct((m, kv_cols), F32)],
        compiler_params=_params(56, 2),
        name="in_proj",
    )(a, w)


def _mm_kernel(a_ref, w_ref, o_ref):
    o_ref[...] = jnp.dot(a_ref[...], w_ref[...], preferred_element_type=F32).astype(o_ref.dtype)


def _mm2_kernel(a1_ref, a2_ref, w1_ref, w2_ref, o_ref):
    o_ref[...] = (jnp.dot(a1_ref[...], w1_ref[...], preferred_element_type=F32)
                  + jnp.dot(a2_ref[...], w2_ref[...], preferred_element_type=F32)).astype(o_ref.dtype)


def _proj(acts, w, tm, tn, vmem_mib, name, casts=()):
    m, k = acts[0].shape
    assert all(a.shape == (m, k) for a in acts) and w.shape[0] == k * len(acts)
    n = w.shape[1]
    tm, tn = _tile(m, tm), _tile(n, tn)
    nj = n // tn
    n_steps = (m // tm) * nj
    cast_in, cast_out, cast_shapes = _cast_specs(casts, n_steps, lambda i, j: i * nj + j)
    body = _mm_kernel if len(acts) == 1 else _mm2_kernel
    out = pl.pallas_call(
        _with_casts(body, 2 * len(acts), 1, len(casts)),
        grid=(m // tm, nj),
        in_specs=([pl.BlockSpec((tm, k), lambda i, j: (i, 0))] * len(acts)
                  + [pl.BlockSpec((k, tn), lambda i, j, r=r: (r, j)) for r in range(len(acts))]
                  + cast_in),
        out_specs=[pl.BlockSpec((tm, tn), lambda i, j: (i, j))] + cast_out,
        out_shape=[jax.ShapeDtypeStruct((m, n), BF16)] + cast_shapes,
        compiler_params=_params(vmem_mib, 2),
        name=name,
    )(*acts, *([w] * len(acts)), *[c[0] for c in casts])
    return out[0], out[1:]


def _sgu_kernel(u_ref, v_ref, w_ref, bt_ref, g_ref, o_ref, *rest, n, groups, with_vn):
    if with_vn:
        vn_ref, acc_ref = rest
    else:
        (acc_ref,) = rest
    rows, d_a = u_ref.shape
    gd = d_a // groups
    qi = lax.broadcasted_iota(jnp.int32, (n, n), 0) // CHUNK
    kj = lax.broadcasted_iota(jnp.int32, (n, n), 1) // CHUNK
    causal = kj <= qi
    sq = [jnp.zeros((n, gd), F32) for _ in range(rows // n)]
    for g in range(groups):
        wg = jnp.where(causal, w_ref[g], 0.0).astype(BF16)
        bias = bt_ref[:, g:g + 1]
        cols = slice(g * gd, (g + 1) * gd)
        for sb in range(rows // n):
            rs = slice(sb * n, (sb + 1) * n)
            vn = _rms(v_ref[rs, cols].astype(F32))
            if with_vn:
                vn_ref[rs, cols] = vn
            s = jnp.dot(wg, vn.astype(BF16), preferred_element_type=F32) + bias
            out = u_ref[rs, cols].astype(F32) * s
            acc_ref[rs, cols] = out
            sq[sb] = sq[sb] + out * out
    for sb in range(rows // n):
        rs = slice(sb * n, (sb + 1) * n)
        inv = lax.rsqrt(jnp.sum(sq[sb], axis=-1, keepdims=True) / d_a + EPS)
        o_ref[rs, :] = (acc_ref[rs, :] * inv * g_ref[...]).astype(BF16)


def _sgu(z, w_s, b_s, g_out, n, with_vn):
    m = z.shape[0]
    d_a = g_out.shape[0]
    groups = w_s.shape[0]
    rows = _tile(m, max(n, SGU_ROWS))
    out_specs = [pl.BlockSpec((rows, d_a), lambda i: (i, 0))]
    out_shape = [jax.ShapeDtypeStruct((m, d_a), BF16)]
    if with_vn:
        out_specs.append(pl.BlockSpec((rows, d_a), lambda i: (i, 0)))
        out_shape.append(jax.ShapeDtypeStruct((m, d_a), F32))
    return pl.pallas_call(
        functools.partial(_sgu_kernel, n=n, groups=groups, with_vn=with_vn),
        grid=(m // rows,),
        in_specs=[pl.BlockSpec((rows, d_a), lambda i: (i, 0)),
                  pl.BlockSpec((rows, d_a), lambda i: (i, 1)),
                  pl.BlockSpec((groups, n, n), lambda i: (0, 0, 0)),
                  pl.BlockSpec((n, groups), lambda i: (0, 0)),
                  pl.BlockSpec((1, d_a), lambda i: (0, 0))],
        out_specs=out_specs, out_shape=out_shape,
        scratch_shapes=[pltpu.VMEM((rows, d_a), F32)],
        compiler_params=_params(40, 1),
        name="sgu_mixer",
    )(z, z, w_s[:, :n, :n], b_s[:, :n].T, g_out.reshape(1, d_a))


def _attend(q_ref, kw, vw, bias, sink_ref, g_ref, o_ref, acc_ref, *, n_kv, hd, phased):
    tq, d_b = q_ref.shape
    span = kw.shape[0]
    rep = d_b // (n_kv * hd)
    assert 2 * hd == LANES and rep % 2 == 0
    kscale = hd ** -0.5 * LOG2E
    zeros = jnp.zeros((span, hd), BF16)
    ones = jnp.ones((span, hd), BF16)
    first = lax.broadcasted_iota(jnp.int32, (tq, LANES), 1) < hd
    nt = (((1,), (1,)), ((), ()))

    def group_operands(g):
        kg = (kw[:, g * hd:(g + 1) * hd] * kscale).astype(BF16)
        vg = vw[:, g * hd:(g + 1) * hd].astype(BF16)
        keys = (jnp.concatenate([kg, zeros], axis=1), jnp.concatenate([zeros, kg], axis=1))
        vals = (jnp.concatenate([vg, zeros, ones, zeros], axis=1),
                jnp.concatenate([zeros, vg, zeros, ones], axis=1))
        return keys, vals

    def logits(h0, key):
        s = lax.dot_general(q_ref[:, h0 * hd:(h0 + 2) * hd], key, nt, preferred_element_type=F32)
        return s if bias is None else s + bias

    def exps(s, h):
        sk = sink_ref[h] * LOG2E
        mx = jnp.maximum(jnp.max(s, axis=-1, keepdims=True), sk)
        return jnp.exp2(s - mx).astype(BF16), jnp.exp2(sk - mx)

    def finish(h0, o0, o1, sink0, sink1):
        out = o0 + o1
        den = out[:, LANES:] + jnp.where(first, sink0, sink1)
        acc_ref[:, h0 * hd:(h0 + 2) * hd] = out[:, :LANES] / den

    pairs = [(g, g * rep + 2 * p) for g in range(n_kv) for p in range(rep // 2)]
    if phased:
        ops = [group_operands(g) for g in range(n_kv)]
        s_all = [[logits(h0, ops[g][0][i]) for i in range(2)] for g, h0 in pairs]
        e_all = [[exps(s[i], h0 + i) for i in range(2)] for s, (g, h0) in zip(s_all, pairs)]
        o_all = [[jnp.dot(e[i][0], ops[g][1][i], preferred_element_type=F32) for i in range(2)]
                 for e, (g, h0) in zip(e_all, pairs)]
        for o, e, (g, h0) in zip(o_all, e_all, pairs):
            finish(h0, o[0], o[1], e[0][1], e[1][1])
    else:
        for g, h0 in pairs:
            if h0 == g * rep:
                keys, vals = group_operands(g)
            e = [exps(logits(h0, keys[i]), h0 + i) for i in range(2)]
            o = [jnp.dot(e[i][0], vals[i], preferred_element_type=F32) for i in range(2)]
            finish(h0, o[0], o[1], e[0][1], e[1][1])
    o_ref[...] = (_rms(acc_ref[...]) * g_ref[...]).astype(BF16)


def _attn_prompt_kernel(q_ref, k_ref, v_ref, sink_ref, g_ref, o_ref, acc_ref, *, window, n_kv, hd):
    tq = q_ref.shape[0]
    span = tq + window
    q0 = pl.program_id(1) * tq
    start = pl.multiple_of(jnp.maximum(q0 - window, 0), CHUNK)
    kw = k_ref[pl.ds(start, span), :]
    vw = v_ref[pl.ds(start, span), :]
    qc = (q0 + lax.broadcasted_iota(jnp.int32, (tq, span), 0)) // CHUNK
    kc = (start + lax.broadcasted_iota(jnp.int32, (tq, span), 1)) // CHUNK
    bias = jnp.where(kc <= qc, jnp.where(kc >= qc - window // CHUNK, 0.0, NEG), NEG)
    _attend(q_ref, kw, vw, bias, sink_ref, g_ref, o_ref, acc_ref, n_kv=n_kv, hd=hd, phased=False)


def _attn_prompt(zm3, zkv3, sink, g_out, window, n_kv, hd):
    bsz, s, _ = zm3.shape
    d_b = g_out.shape[0]
    kvw = n_kv * hd
    tq = _tile(s, ATTN_TQ)
    assert tq % CHUNK == 0 and window % CHUNK == 0 and tq + window <= s
    return pl.pallas_call(
        functools.partial(_attn_prompt_kernel, window=window, n_kv=n_kv, hd=hd),
        grid=(bsz, s // tq),
        in_specs=[pl.BlockSpec((None, tq, d_b), lambda b, t: (b, t, 2)),
                  pl.BlockSpec((None, s, kvw), lambda b, t: (b, 0, 0)),
                  pl.BlockSpec((None, s, kvw), lambda b, t: (b, 0, 1)),
                  pl.BlockSpec(memory_space=pltpu.SMEM),
                  pl.BlockSpec((1, d_b), lambda b, t: (0, 0))],
        out_specs=pl.BlockSpec((None, tq, d_b), lambda b, t: (b, t, 0)),
        out_shape=jax.ShapeDtypeStruct((bsz, s, d_b), BF16),
        scratch_shapes=[pltpu.VMEM((tq, d_b), F32)],
        compiler_params=_params(40, 2),
        name="attn_prompt",
    )(zm3, zkv3, zkv3, sink, g_out.reshape(1, d_b))


def _attn_sample_kernel(q_ref, kn_ref, vn_ref, ck_ref, cv_ref, sink_ref, g_ref, o_ref, acc_ref, *, n_kv, hd):
    kw = jnp.concatenate([ck_ref[...], kn_ref[...]], axis=0)
    vw = jnp.concatenate([cv_ref[...], vn_ref[...]], axis=0)
    _attend(q_ref, kw, vw, None, sink_ref, g_ref, o_ref, acc_ref, n_kv=n_kv, hd=hd, phased=True)


def _attn_sample(zm3, zkv3, ck, cv, sink, g_out, n_kv, hd):
    bsz, t, _ = zm3.shape
    d_b = g_out.shape[0]
    kvw = n_kv * hd
    win = ck.shape[1]
    return pl.pallas_call(
        functools.partial(_attn_sample_kernel, n_kv=n_kv, hd=hd),
        grid=(bsz,),
        in_specs=[pl.BlockSpec((None, t, d_b), lambda b: (b, 0, 2)),
                  pl.BlockSpec((None, t, kvw), lambda b: (b, 0, 0)),
                  pl.BlockSpec((None, t, kvw), lambda b: (b, 0, 1)),
                  pl.BlockSpec((None, win, kvw), lambda b: (b, 0, 0)),
                  pl.BlockSpec((None, win, kvw), lambda b: (b, 0, 0)),
                  pl.BlockSpec(memory_space=pltpu.SMEM),
                  pl.BlockSpec((1, d_b), lambda b: (0, 0))],
        out_specs=pl.BlockSpec((None, t, d_b), lambda b: (b, 0, 0)),
        out_shape=jax.ShapeDtypeStruct((bsz, t, d_b), BF16),
        scratch_shapes=[pltpu.VMEM((t, d_b), F32)],
        compiler_params=_params(32, 1),
        name="attn_sample",
    )(zm3, zkv3, zkv3, ck.reshape(bsz, win, kvw), cv.reshape(bsz, win, kvw), sink, g_out.reshape(1, d_b))


def _conv_gate(a, b, prev, cw, cb):
    rows = a.shape[0]
    row = lax.broadcasted_iota(jnp.int32, a.shape, 0)
    a1 = jnp.where(row == 0, prev[1:2], pltpu.roll(a, 1, 0))
    a2 = jnp.where(row == 0, prev[0:1], jnp.where(row == 1, prev[1:2], pltpu.roll(a, 2, 0)))
    conv = cw[0:1] * a2 + cw[1:2] * a1 + cw[2:3] * a + cb
    return (jax.nn.silu(conv) * b).astype(BF16), a[rows - 2:rows]


def _up_prompt_kernel(h_ref, wa_ref, wb_ref, cw_ref, cb_ref, act_ref, nc_ref):
    tm = h_ref.shape[0]
    tn = wa_ref.shape[1]
    cols = pl.ds(pl.multiple_of(pl.program_id(1) * tn, tn), tn)
    cw, cb = cw_ref[:, cols], cb_ref[:, cols]
    last = jnp.zeros((2, tn), F32)
    rc = _tile(tm, UP_RC)
    for c in range(tm // rc):
        rs = slice(c * rc, (c + 1) * rc)
        h = h_ref[rs, :]
        a = jnp.dot(h, wa_ref[...], preferred_element_type=F32)
        b = jnp.dot(h, wb_ref[...], preferred_element_type=F32)
        act_ref[rs, :], last = _conv_gate(a, b, last, cw, cb)
    nc_ref[:, cols] = last


def _up_proj_prompt(h, w_up, conv_w, conv_b, bsz, casts=()):
    m, d = h.shape
    d_ff = conv_w.shape[1]
    tm = m // bsz
    tn = _tile(d_ff, UP_TN)
    nj = d_ff // tn
    cast_in, cast_out, cast_shapes = _cast_specs(casts, bsz * nj, lambda i, j: i * nj + j)
    out = pl.pallas_call(
        _with_casts(_up_prompt_kernel, 5, 2, len(casts)),
        grid=(bsz, nj),
        in_specs=[pl.BlockSpec((tm, d), lambda i, j: (i, 0)),
                  pl.BlockSpec((d, tn), lambda i, j: (0, j)),
                  pl.BlockSpec((d, tn), lambda i, j: (0, nj + j)),
                  pl.BlockSpec((conv_w.shape[0], d_ff), lambda i, j: (0, 0)),
                  pl.BlockSpec((1, d_ff), lambda i, j: (0, 0))] + cast_in,
        out_specs=[pl.BlockSpec((tm, tn), lambda i, j: (i, j)),
                   pl.BlockSpec((None, 2, d_ff), lambda i, j: (i, 0, 0))] + cast_out,
        out_shape=[jax.ShapeDtypeStruct((m, d_ff), BF16),
                   jax.ShapeDtypeStruct((bsz, 2, d_ff), F32)] + cast_shapes,
        compiler_params=_params(56, 2),
        name="up_proj_conv_gate",
    )(h, w_up, w_up, conv_w, conv_b.reshape(1, d_ff), *[c[0] for c in casts])
    return out[0], out[1], out[2:]


def _up_sample_kernel(h_ref, wa_ref, wb_ref, cw_ref, cb_ref, st_ref, act_ref, nc_ref):
    h = h_ref[...]
    a = jnp.dot(h, wa_ref[...], preferred_element_type=F32)
    b = jnp.dot(h, wb_ref[...], preferred_element_type=F32)
    cw, cb = cw_ref[...], cb_ref[...]
    pieces = st_ref.shape[0]
    t = h.shape[0] // pieces
    for p in range(pieces):
        rs = slice(p * t, (p + 1) * t)
        act_ref[rs, :], nc_ref[p] = _conv_gate(a[rs], b[rs], st_ref[p], cw, cb)


def _up_proj_sample(h, w_up, conv_w, conv_b, state):
    m, d = h.shape
    bsz = state.shape[0]
    d_ff = conv_w.shape[1]
    tn = _tile(d_ff, UP_TN)
    nj = d_ff // tn
    return pl.pallas_call(
        _up_sample_kernel,
        grid=(nj,),
        in_specs=[pl.BlockSpec((m, d), lambda j: (0, 0)),
                  pl.BlockSpec((d, tn), lambda j: (0, j)),
                  pl.BlockSpec((d, tn), lambda j: (0, nj + j)),
                  pl.BlockSpec((conv_w.shape[0], tn), lambda j: (0, j)),
                  pl.BlockSpec((1, tn), lambda j: (0, j)),
                  pl.BlockSpec((bsz, 2, tn), lambda j: (0, 0, j))],
        out_specs=[pl.BlockSpec((m, tn), lambda j: (0, j)),
                   pl.BlockSpec((bsz, 2, tn), lambda j: (0, 0, j))],
        out_shape=[jax.ShapeDtypeStruct((m, d_ff), BF16), jax.ShapeDtypeStruct((bsz, 2, d_ff), F32)],
        compiler_params=_params(32, 1),
        name="up_proj_conv_gate_sample",
    )(h, w_up, w_up, conv_w, conv_b.reshape(1, d_ff), state)


class _Weights:
    def __init__(self, depth):
        self.w_in = [None] * depth
        self.w_out = [None] * depth
        self.w_up = [None] * depth
        self.w_down = [None] * depth


def _trunk(x, mods, boff, p, wb, raw, cache, state):
    bsz, seq, d = x.shape
    depth = p["w_in"].shape[0]
    m = bsz * seq
    d_a, d_b = p["g_out_a"].shape[1], p["g_out_b"].shape[1]
    window, n_kv, hd = cache[0].shape[2:]
    kvw = n_kv * hd
    sample = raw is None
    n_blk = min(seq, p["sgu_w"].shape[2])
    ks, vs, convs, sgus = [], [], [], []
    h = _prenorm(x, p["g_pre_mix"][0], mods[0], boff, 1, 0)
    for l in range(depth):
        zm, zkv = _in_proj(h.reshape(m, d), wb.w_in[l], 2 * kvw)
        zm3, zkv3 = zm.reshape(bsz, seq, 3 * d_b), zkv.reshape(bsz, seq, 2 * kvw)
        kept = seq if sample else window
        tail = zkv3[:, seq - kept:, :]
        ks.append(tail[:, :, :kvw].reshape(bsz, kept, n_kv, hd))
        vs.append(tail[:, :, kvw:].reshape(bsz, kept, n_kv, hd))
        if sample:
            a_n, vn = _sgu(zm, p["sgu_w"][l], p["sgu_b"][l], p["g_out_a"][l], n_blk, True)
            b_n = _attn_sample(zm3, zkv3, cache[0][l], cache[1][l], p["attn_sink"][l], p["g_out_b"][l], n_kv, hd)
            sgus.append(vn.reshape(bsz, seq, d_a))
        else:
            (a_n,) = _sgu(zm, p["sgu_w"][l], p["sgu_b"][l], p["g_out_a"][l], n_blk, False)
            b_n = _attn_prompt(zm3, zkv3, p["attn_sink"][l], p["g_out_b"][l], window, n_kv, hd)
        up_cast = ((raw[2], l, 1),) if not sample and l == 0 else ()
        mix, done = _proj([a_n, b_n.reshape(m, d_b)], wb.w_out[l], OUT_TM,
                          OUT_TN_HOST if up_cast else OUT_TN, 48, "out_proj", up_cast)
        if up_cast:
            (wb.w_up[l],) = done
        x, h = _resid(x, mix, p["g_post_mix"][l], mods[l], boff, 2, (p["g_pre_ffn"][l], mods[l], 4, 3))
        if sample:
            act, new_conv = _up_proj_sample(h.reshape(m, d), wb.w_up[l], p["conv_w"][l], p["conv_b"][l], state[l])
        else:
            last_layer = l + 1 == depth
            casts = ((raw[3], l, 1),) + (() if last_layer else ((raw[2], l + 1, raw[2].shape[2] // (2 * UP_TN)),))
            act, new_conv, done = _up_proj_prompt(
                h.reshape(m, d), wb.w_up[l], p["conv_w"][l], p["conv_b"][l], bsz, casts)
            wb.w_down[l] = done[0]
            if not last_layer:
                wb.w_up[l + 1] = done[1]
        convs.append(new_conv)
        nxt_casts = () if sample or l + 1 == depth else ((raw[0], l + 1, 1), (raw[1], l + 1, 1))
        f, done = _proj([act], wb.w_down[l], DOWN_TM, DOWN_TN, 56, "down_proj", nxt_casts)
        if nxt_casts:
            wb.w_in[l + 1], wb.w_out[l + 1] = done
        nxt = (p["g_pre_mix"][l + 1], mods[l + 1], 1, 0) if l + 1 < depth else None
        x, h = _resid(x, f, p["g_post_ffn"][l], mods[l], boff, 5, nxt)
    sgu = jnp.stack(sgus) if sample else None
    return x, jnp.stack(ks), jnp.stack(vs), jnp.stack(convs), sgu


def kernel(x_prompt, x_sample, c_prompt, c_sample, cache_k, cache_v, state_conv, w_mod, b_mod, g_pre_mix, g_post_mix, w_in, sgu_w, sgu_b, attn_sink, g_out_a, g_out_b, w_out, g_pre_ffn, g_post_ffn, w_up, conv_w, conv_b, w_down):
    depth, d = g_pre_mix.shape
    nb_p = c_prompt.shape[0]
    c_all = jnp.concatenate([c_prompt, c_sample], axis=0)
    mod = _modulation(c_all, w_mod, b_mod).reshape(depth, c_all.shape[0], N_MOD, 1, d)
    mods = [mod[l] for l in range(depth)]
    p = dict(g_pre_mix=g_pre_mix, g_post_mix=g_post_mix, w_in=w_in, sgu_w=sgu_w, sgu_b=sgu_b,
             attn_sink=attn_sink, g_out_a=g_out_a, g_out_b=g_out_b, g_pre_ffn=g_pre_ffn,
             g_post_ffn=g_post_ffn, conv_w=conv_w, conv_b=conv_b)
    wb = _Weights(depth)
    wb.w_in[0] = w_in[0].astype(BF16)
    wb.w_out[0] = w_out[0].astype(BF16)
    cache = (cache_k, cache_v)
    y_p, k_p, v_p, conv_p, _ = _trunk(x_prompt, mods, 0, p, wb, (w_in, w_out, w_up, w_down), cache, None)
    y_s, k_s, v_s, conv_s, sgu_s = _trunk(x_sample, mods, nb_p, p, wb, None, cache, state_conv)
    return (y_p, y_s, k_p, v_p, k_s, v_s, conv_p, conv_s, sgu_s)
```

```python
import functools

import jax
import jax.numpy as jnp
from jax import lax
from jax.experimental import pallas as pl
from jax.experimental.pallas import tpu as pltpu

F32 = jnp.float32
BF16 = jnp.bfloat16

CHUNK = 64
EPS = 1e-6
NEG = -1e30
N_MOD = 6
LOG2E = 1.4426950408889634
LANES = 128
BF16_ROWS = 16

MOD_TN = 512
NORM_TS = 512
RESID_TS = 256
PROJ_TM = 2048
OUT_TM, OUT_TN = 1024, 1024
OUT_TN_HOST = 512
UP_TN = 256
UP_RC = 1024
DOWN_TM, DOWN_TN = 512, 512
SGU_ROWS = 512
ATTN_TQ = 128
MIB = 1024 * 1024


def _tile(full, pref):
    t = min(full, pref)
    assert full % t == 0, (full, pref)
    return t


def _params(vmem_mib, n_axes):
    return pltpu.CompilerParams(dimension_semantics=("arbitrary",) * n_axes,
                                vmem_limit_bytes=vmem_mib * MIB)


def _rms(x):
    return x * lax.rsqrt(jnp.mean(x * x, axis=-1, keepdims=True) + EPS)


def _with_casts(body, n_in, n_out, n_cast):
    if not n_cast:
        return body

    def kernel(*refs):
        ins = refs[:n_in]
        cast_in = refs[n_in:n_in + n_cast]
        outs = refs[n_in + n_cast:n_in + n_cast + n_out]
        cast_out = refs[n_in + n_cast + n_out:n_in + 2 * n_cast + n_out]
        for src, dst in zip(cast_in, cast_out):
            dst[...] = src[...].astype(BF16)
        body(*ins, *outs, *refs[n_in + 2 * n_cast + n_out:])

    return kernel


def _cast_specs(casts, n_steps, step_of):
    in_specs, out_specs, shapes = [], [], []
    for w, l, cb in casts:
        _, rows, cols = w.shape
        assert n_steps % cb == 0 and cols % cb == 0
        slab, width = rows // (n_steps // cb), cols // cb
        assert slab * n_steps == rows * cb and slab % BF16_ROWS == 0 and (width % LANES == 0 or cb == 1), (
            w.shape, n_steps, cb)
        in_specs.append(pl.BlockSpec((None, slab, width),
                                     lambda *g, l=l, cb=cb: (l, step_of(*g) // cb, step_of(*g) % cb)))
        out_specs.append(pl.BlockSpec((slab, width), lambda *g, cb=cb: (step_of(*g) // cb, step_of(*g) % cb)))
        shapes.append(jax.ShapeDtypeStruct((rows, cols), BF16))
    return in_specs, out_specs, shapes


def _mod_kernel(c_ref, w_ref, b_ref, o_ref):
    act = jax.nn.silu(c_ref[...]).astype(BF16)
    o_ref[...] = jnp.dot(act, w_ref[...].astype(BF16), preferred_element_type=F32) + b_ref[...]


def _modulation(c, w_mod, b_mod):
    depth, d, n = w_mod.shape
    rows = c.shape[0]
    tn = _tile(n, MOD_TN)
    return pl.pallas_call(
        _mod_kernel,
        grid=(depth, n // tn),
        in_specs=[pl.BlockSpec((rows, d), lambda l, j: (0, 0)),
                  pl.BlockSpec((None, d, tn), lambda l, j: (l, 0, j)),
                  pl.BlockSpec((None, 1, tn), lambda l, j: (l, 0, j))],
        out_specs=pl.BlockSpec((None, rows, tn), lambda l, j: (l, 0, j)),
        out_shape=jax.ShapeDtypeStruct((depth, rows, n), F32),
        compiler_params=_params(40, 2),
        name="modulation",
    )(c, w_mod, b_mod.reshape(depth, 1, n))


def _prenorm_kernel(x_ref, g_ref, sc_ref, sh_ref, h_ref):
    y = _rms(x_ref[...]) * g_ref[...]
    h_ref[...] = (y * (1 + sc_ref[...]) + sh_ref[...]).astype(BF16)


def _mod_spec(boff, j, d):
    return pl.BlockSpec((None, None, 1, d), lambda b, t: (boff + b, j, 0, 0))


def _prenorm(x, g, mod, boff, j_sc, j_sh):
    bsz, s, d = x.shape
    ts = _tile(s, NORM_TS)
    return pl.pallas_call(
        _prenorm_kernel,
        grid=(bsz, s // ts),
        in_specs=[pl.BlockSpec((None, ts, d), lambda b, t: (b, t, 0)),
                  pl.BlockSpec((1, d), lambda b, t: (0, 0)),
                  _mod_spec(boff, j_sc, d), _mod_spec(boff, j_sh, d)],
        out_specs=pl.BlockSpec((None, ts, d), lambda b, t: (b, t, 0)),
        out_shape=jax.ShapeDtypeStruct((bsz, s, d), BF16),
        compiler_params=_params(32, 2),
        name="prenorm",
    )(x, g.reshape(1, d), mod, mod)


def _resid_kernel(x_ref, m_ref, gt_ref, gpost_ref, *rest, with_next):
    xn = x_ref[...] + gt_ref[...] * (_rms(m_ref[...].astype(F32)) * gpost_ref[...])
    if with_next:
        gpre_ref, sc_ref, sh_ref, xo_ref, h_ref = rest
        y = _rms(xn) * gpre_ref[...]
        h_ref[...] = (y * (1 + sc_ref[...]) + sh_ref[...]).astype(BF16)
    else:
        (xo_ref,) = rest
    xo_ref[...] = xn


def _resid(x, m, g_post, mod, boff, j_gt, nxt=None):
    bsz, s, d = x.shape
    ts = _tile(s, RESID_TS)
    row = pl.BlockSpec((None, ts, d), lambda b, t: (b, t, 0))
    vec = pl.BlockSpec((1, d), lambda b, t: (0, 0))
    in_specs = [row, row, _mod_spec(boff, j_gt, d), vec]
    args = [x, m.reshape(bsz, s, d), mod, g_post.reshape(1, d)]
    out_specs = [row]
    out_shape = [jax.ShapeDtypeStruct((bsz, s, d), F32)]
    if nxt is not None:
        g_pre, mod_next, j_sc, j_sh = nxt
        in_specs += [vec, _mod_spec(boff, j_sc, d), _mod_spec(boff, j_sh, d)]
        args += [g_pre.reshape(1, d), mod_next, mod_next]
        out_specs.append(row)
        out_shape.append(jax.ShapeDtypeStruct((bsz, s, d), BF16))
    out = pl.pallas_call(
        functools.partial(_resid_kernel, with_next=nxt is not None),
        grid=(bsz, s // ts),
        in_specs=in_specs, out_specs=out_specs, out_shape=out_shape,
        compiler_params=_params(48, 2),
        name="resid_norm",
    )(*args)
    return out if nxt is not None else (out[0], None)


def _in_proj_kernel(a_ref, w_ref, zm_ref, zkv_ref):
    r = jnp.dot(a_ref[...], w_ref[...], preferred_element_type=F32)
    zm_ref[...] = r.astype(BF16)

    @pl.when(pl.program_id(1) == 0)
    def _():
        zkv_ref[...] = r


def _in_proj(a, w, kv_cols):
    m, k = a.shape
    n = w.shape[1]
    tm, tn = _tile(m, PROJ_TM), kv_cols
    assert (n - kv_cols) % tn == 0
    n_main = (n - kv_cols) // tn
    return pl.pallas_call(
        _in_proj_kernel,
        grid=(m // tm, n_main + 1),
        in_specs=[pl.BlockSpec((tm, k), lambda i, j: (i, 0)),
                  pl.BlockSpec((k, tn), lambda i, j: (0, jnp.where(j == 0, n_main, j - 1)))],
        out_specs=[pl.BlockSpec((tm, tn), lambda i, j: (i, jnp.maximum(j - 1, 0))),
                   pl.BlockSpec((tm, tn), lambda i, j: (i, 0))],
        out_shape=[jax.ShapeDtypeStruct((m, n - kv_cols), BF16), jax.ShapeDtypeStruct((m, kv_cols), F32)],
        compiler_params=_params(56, 2),
        name="in_proj",
    )(a, w)


def _mm_kernel(a_ref, w_ref, o_ref):
    o_ref[...] = jnp.dot(a_ref[...], w_ref[...], preferred_element_type=F32).astype(o_ref.dtype)


def _mm2_kernel(a1_ref, a2_ref, w1_ref, w2_ref, o_ref):
    o_ref[...] = (jnp.dot(a1_ref[...], w1_ref[...], preferred_element_type=F32)
                  + jnp.dot(a2_ref[...], w2_ref[...], preferred_element_type=F32)).astype(o_ref.dtype)


def _proj(acts, w, tm, tn, vmem_mib, name, casts=()):
    m, k = acts[0].shape
    assert all(a.shape == (m, k) for a in acts) and w.shape[0] == k * len(acts)
    n = w.shape[1]
    tm, tn = _tile(m, tm), _tile(n, tn)
    nj = n // tn
    n_steps = (m // tm) * nj
    cast_in, cast_out, cast_shapes = _cast_specs(casts, n_steps, lambda i, j: i * nj + j)
    body = _mm_kernel if len(acts) == 1 else _mm2_kernel
    out = pl.pallas_call(
        _with_casts(body, 2 * len(acts), 1, len(casts)),
        grid=(m // tm, nj),
        in_specs=([pl.BlockSpec((tm, k), lambda i, j: (i, 0))] * len(acts)
                  + [pl.BlockSpec((k, tn), lambda i, j, r=r: (r, j)) for r in range(len(acts))]
                  + cast_in),
        out_specs=[pl.BlockSpec((tm, tn), lambda i, j: (i, j))] + cast_out,
        out_shape=[jax.ShapeDtypeStruct((m, n), BF16)] + cast_shapes,
        compiler_params=_params(vmem_mib, 2),
        name=name,
    )(*acts, *([w] * len(acts)), *[c[0] for c in casts])
    return out[0], out[1:]


def _sgu_kernel(u_ref, v_ref, w_ref, bt_ref, g_ref, o_ref, *rest, n, groups, with_vn):
    if with_vn:
        vn_ref, acc_ref = rest
    else:
        (acc_ref,) = rest
    rows, d_a = u_ref.shape
    gd = d_a // groups
    qi = lax.broadcasted_iota(jnp.int32, (n, n), 0) // CHUNK
    kj = lax.broadcasted_iota(jnp.int32, (n, n), 1) // CHUNK
    causal = kj <= qi
    sq = [jnp.zeros((n, gd), F32) for _ in range(rows // n)]
    for g in range(groups):
        wg = jnp.where(causal, w_ref[g], 0.0).astype(BF16)
        bias = bt_ref[:, g:g + 1]
        cols = slice(g * gd, (g + 1) * gd)
        for sb in range(rows // n):
            rs = slice(sb * n, (sb + 1) * n)
            vn = _rms(v_ref[rs, cols].astype(F32))
            if with_vn:
                vn_ref[rs, cols] = vn
            s = jnp.dot(wg, vn.astype(BF16), preferred_element_type=F32) + bias
            out = u_ref[rs, cols].astype(F32) * s
            acc_ref[rs, cols] = out
            sq[sb] = sq[sb] + out * out
    for sb in range(rows // n):
        rs = slice(sb * n, (sb + 1) * n)
        inv = lax.rsqrt(jnp.sum(sq[sb], axis=-1, keepdims=True) / d_a + EPS)
        o_ref[rs, :] = (acc_ref[rs, :] * inv * g_ref[...]).astype(BF16)


def _sgu(z, w_s, b_s, g_out, n, with_vn):
    m = z.shape[0]
    d_a = g_out.shape[0]
    groups = w_s.shape[0]
    rows = _tile(m, max(n, SGU_ROWS))
    out_specs = [pl.BlockSpec((rows, d_a), lambda i: (i, 0))]
    out_shape = [jax.ShapeDtypeStruct((m, d_a), BF16)]
    if with_vn:
        out_specs.append(pl.BlockSpec((rows, d_a), lambda i: (i, 0)))
        out_shape.append(jax.ShapeDtypeStruct((m, d_a), F32))
    return pl.pallas_call(
        functools.partial(_sgu_kernel, n=n, groups=groups, with_vn=with_vn),
        grid=(m // rows,),
        in_specs=[pl.BlockSpec((rows, d_a), lambda i: (i, 0)),
                  pl.BlockSpec((rows, d_a), lambda i: (i, 1)),
                  pl.BlockSpec((groups, n, n), lambda i: (0, 0, 0)),
                  pl.BlockSpec((n, groups), lambda i: (0, 0)),
                  pl.BlockSpec((1, d_a), lambda i: (0, 0))],
        out_specs=out_specs, out_shape=out_shape,
        scratch_shapes=[pltpu.VMEM((rows, d_a), F32)],
        compiler_params=_params(40, 1),
        name="sgu_mixer",
    )(z, z, w_s[:, :n, :n], b_s[:, :n].T, g_out.reshape(1, d_a))


def _attend(q_ref, kw, vw, bias, sink_ref, g_ref, o_ref, acc_ref, *, n_kv, hd, phased):
    tq, d_b = q_ref.shape
    span = kw.shape[0]
    rep = d_b // (n_kv * hd)
    assert 2 * hd == LANES and rep % 2 == 0
    kscale = hd ** -0.5 * LOG2E
    zeros = jnp.zeros((span, hd), BF16)
    ones = jnp.ones((span, hd), BF16)
    first = lax.broadcasted_iota(jnp.int32, (tq, LANES), 1) < hd
    nt = (((1,), (1,)), ((), ()))

    def group_operands(g):
        kg = (kw[:, g * hd:(g + 1) * hd] * kscale).astype(BF16)
        vg = vw[:, g * hd:(g + 1) * hd].astype(BF16)
        keys = (jnp.concatenate([kg, zeros], axis=1), jnp.concatenate([zeros, kg], axis=1))
        vals = (jnp.concatenate([vg, zeros, ones, zeros], axis=1),
                jnp.concatenate([zeros, vg, zeros, ones], axis=1))
        return keys, vals

    def logits(h0, key):
        s = lax.dot_general(q_ref[:, h0 * hd:(h0 + 2) * hd], key, nt, preferred_element_type=F32)
        return s if bias is None else s + bias

    def exps(s, h):
        sk = sink_ref[h] * LOG2E
        mx = jnp.maximum(jnp.max(s, axis=-1, keepdims=True), sk)
        return jnp.exp2(s - mx).astype(BF16), jnp.exp2(sk - mx)

    def finish(h0, o0, o1, sink0, sink1):
        out = o0 + o1
        den = out[:, LANES:] + jnp.where(first, sink0, sink1)
        acc_ref[:, h0 * hd:(h0 + 2) * hd] = out[:, :LANES] / den

    pairs = [(g, g * rep + 2 * p) for g in range(n_kv) for p in range(rep // 2)]
    if phased:
        ops = [group_operands(g) for g in range(n_kv)]
        s_all = [[logits(h0, ops[g][0][i]) for i in range(2)] for g, h0 in pairs]
        e_all = [[exps(s[i], h0 + i) for i in range(2)] for s, (g, h0) in zip(s_all, pairs)]
        o_all = [[jnp.dot(e[i][0], ops[g][1][i], preferred_element_type=F32) for i in range(2)]
                 for e, (g, h0) in zip(e_all, pairs)]
        for o, e, (g, h0) in zip(o_all, e_all, pairs):
            finish(h0, o[0], o[1], e[0][1], e[1][1])
    else:
        for g, h0 in pairs:
            if h0 == g * rep:
                keys, vals = group_operands(g)
            e = [exps(logits(h0, keys[i]), h0 + i) for i in range(2)]
            o = [jnp.dot(e[i][0], vals[i], preferred_element_type=F32) for i in range(2)]
            finish(h0, o[0], o[1], e[0][1], e[1][1])
    o_ref[...] = (_rms(acc_ref[...]) * g_ref[...]).astype(BF16)


def _attn_prompt_kernel(q_ref, k_ref, v_ref, sink_ref, g_ref, o_ref, acc_ref, *, window, n_kv, hd):
    tq = q_ref.shape[0]
    span = tq + window
    q0 = pl.program_id(1) * tq
    start = pl.multiple_of(jnp.maximum(q0 - window, 0), CHUNK)
    kw = k_ref[pl.ds(start, span), :]
    vw = v_ref[pl.ds(start, span), :]
    qc = (q0 + lax.broadcasted_iota(jnp.int32, (tq, span), 0)) // CHUNK
    kc = (start + lax.broadcasted_iota(jnp.int32, (tq, span), 1)) // CHUNK
    bias = jnp.where(kc <= qc, jnp.where(kc >= qc - window // CHUNK, 0.0, NEG), NEG)
    _attend(q_ref, kw, vw, bias, sink_ref, g_ref, o_ref, acc_ref, n_kv=n_kv, hd=hd, phased=False)


def _attn_prompt(zm3, zkv3, sink, g_out, window, n_kv, hd):
    bsz, s, _ = zm3.shape
    d_b = g_out.shape[0]
    kvw = n_kv * hd
    tq = _tile(s, ATTN_TQ)
    assert tq % CHUNK == 0 and window % CHUNK == 0 and tq + window <= s
    return pl.pallas_call(
        functools.partial(_attn_prompt_kernel, window=window, n_kv=n_kv, hd=hd),
        grid=(bsz, s // tq),
        in_specs=[pl.BlockSpec((None, tq, d_b), lambda b, t: (b, t, 2)),
                  pl.BlockSpec((None, s, kvw), lambda b, t: (b, 0, 0)),
                  pl.BlockSpec((None, s, kvw), lambda b, t: (b, 0, 1)),
                  pl.BlockSpec(memory_space=pltpu.SMEM),
                  pl.BlockSpec((1, d_b), lambda b, t: (0, 0))],
        out_specs=pl.BlockSpec((None, tq, d_b), lambda b, t: (b, t, 0)),
        out_shape=jax.ShapeDtypeStruct((bsz, s, d_b), BF16),
        scratch_shapes=[pltpu.VMEM((tq, d_b), F32)],
        compiler_params=_params(40, 2),
        name="attn_prompt",
    )(zm3, zkv3, zkv3, sink, g_out.reshape(1, d_b))


def _attn_sample_kernel(q_ref, kn_ref, vn_ref, ck_ref, cv_ref, sink_ref, g_ref, o_ref, acc_ref, *, n_kv, hd):
    kw = jnp.concatenate([ck_ref[...], kn_ref[...]], axis=0)
    vw = jnp.concatenate([cv_ref[...], vn_ref[...]], axis=0)
    _attend(q_ref, kw, vw, None, sink_ref, g_ref, o_ref, acc_ref, n_kv=n_kv, hd=hd, phased=True)


def _attn_sample(zm3, zkv3, ck, cv, sink, g_out, n_kv, hd):
    bsz, t, _ = zm3.shape
    d_b = g_out.shape[0]
    kvw = n_kv * hd
    win = ck.shape[1]
    return pl.pallas_call(
        functools.partial(_attn_sample_kernel, n_kv=n_kv, hd=hd),
        grid=(bsz,),
        in_specs=[pl.BlockSpec((None, t, d_b), lambda b: (b, 0, 2)),
                  pl.BlockSpec((None, t, kvw), lambda b: (b, 0, 0)),
                  pl.BlockSpec((None, t, kvw), lambda b: (b, 0, 1)),
                  pl.BlockSpec((None, win, kvw), lambda b: (b, 0, 0)),
                  pl.BlockSpec((None, win, kvw), lambda b: (b, 0, 0)),
                  pl.BlockSpec(memory_space=pltpu.SMEM),
                  pl.BlockSpec((1, d_b), lambda b: (0, 0))],
        out_specs=pl.BlockSpec((None, t, d_b), lambda b: (b, 0, 0)),
        out_shape=jax.ShapeDtypeStruct((bsz, t, d_b), BF16),
        scratch_shapes=[pltpu.VMEM((t, d_b), F32)],
        compiler_params=_params(32, 1),
        name="attn_sample",
    )(zm3, zkv3, zkv3, ck.reshape(bsz, win, kvw), cv.reshape(bsz, win, kvw), sink, g_out.reshape(1, d_b))


def _conv_silu(a, prev, cw, cb):
    rows = a.shape[0]
    row = lax.broadcasted_iota(jnp.int32, a.shape, 0)
    a1 = jnp.where(row == 0, prev[1:2], pltpu.roll(a, 1, 0))
    a2 = jnp.where(row == 0, prev[0:1], jnp.where(row == 1, prev[1:2], pltpu.roll(a, 2, 0)))
    conv = cw[0:1] * a2 + cw[1:2] * a1 + cw[2:3] * a + cb
    return jax.nn.silu(conv), a[rows - 2:rows]


def _up_prompt_kernel(h_ref, wa_ref, wb_ref, cw_ref, cb_ref, act_ref, nc_ref):
    tm = h_ref.shape[0]
    tn = wa_ref.shape[1]
    cols = pl.ds(pl.multiple_of(pl.program_id(1) * tn, tn), tn)
    cw, cb = cw_ref[:, cols], cb_ref[:, cols]
    last = jnp.zeros((2, tn), F32)
    rc = _tile(tm, UP_RC)
    for c in range(tm // rc):
        rs = slice(c * rc, (c + 1) * rc)
        h = h_ref[rs, :]
        a = jnp.dot(h, wa_ref[...], preferred_element_type=F32)
        gate, last = _conv_silu(a, last, cw, cb)
        b = jnp.dot(h, wb_ref[...], preferred_element_type=F32)
        act_ref[rs, :] = (gate * b).astype(BF16)
    nc_ref[:, cols] = last


def _up_proj_prompt(h, w_up, conv_w, conv_b, bsz, casts=()):
    m, d = h.shape
    d_ff = conv_w.shape[1]
    tm = m // bsz
    tn = _tile(d_ff, UP_TN)
    nj = d_ff // tn
    cast_in, cast_out, cast_shapes = _cast_specs(casts, bsz * nj, lambda i, j: i * nj + j)
    out = pl.pallas_call(
        _with_casts(_up_prompt_kernel, 5, 2, len(casts)),
        grid=(bsz, nj),
        in_specs=[pl.BlockSpec((tm, d), lambda i, j: (i, 0)),
                  pl.BlockSpec((d, tn), lambda i, j: (0, j)),
                  pl.BlockSpec((d, tn), lambda i, j: (0, nj + j)),
                  pl.BlockSpec((conv_w.shape[0], d_ff), lambda i, j: (0, 0)),
                  pl.BlockSpec((1, d_ff), lambda i, j: (0, 0))] + cast_in,
        out_specs=[pl.BlockSpec((tm, tn), lambda i, j: (i, j)),
                   pl.BlockSpec((None, 2, d_ff), lambda i, j: (i, 0, 0))] + cast_out,
        out_shape=[jax.ShapeDtypeStruct((m, d_ff), BF16),
                   jax.ShapeDtypeStruct((bsz, 2, d_ff), F32)] + cast_shapes,
        compiler_params=_params(56, 2),
        name="up_proj_conv_gate",
    )(h, w_up, w_up, conv_w, conv_b.reshape(1, d_ff), *[c[0] for c in casts])
    return out[0], out[1], out[2:]


def _up_sample_kernel(h_ref, wa_ref, wb_ref, cw_ref, cb_ref, st_ref, act_ref, nc_ref):
    h = h_ref[...]
    a = jnp.dot(h, wa_ref[...], preferred_element_type=F32)
    b = jnp.dot(h, wb_ref[...], preferred_element_type=F32)
    cw, cb = cw_ref[...], cb_ref[...]
    pieces = st_ref.shape[0]
    t = h.shape[0] // pieces
    for p in range(pieces):
        rs = slice(p * t, (p + 1) * t)
        gate, nc_ref[p] = _conv_silu(a[rs], st_ref[p], cw, cb)
        act_ref[rs, :] = (gate * b[rs]).astype(BF16)


def _up_proj_sample(h, w_up, conv_w, conv_b, state):
    m, d = h.shape
    bsz = state.shape[0]
    d_ff = conv_w.shape[1]
    tn = _tile(d_ff, UP_TN)
    nj = d_ff // tn
    return pl.pallas_call(
        _up_sample_kernel,
        grid=(nj,),
        in_specs=[pl.BlockSpec((m, d), lambda j: (0, 0)),
                  pl.BlockSpec((d, tn), lambda j: (0, j)),
                  pl.BlockSpec((d, tn), lambda j: (0, nj + j)),
                  pl.BlockSpec((conv_w.shape[0], tn), lambda j: (0, j)),
                  pl.BlockSpec((1, tn), lambda j: (0, j)),
                  pl.BlockSpec((bsz, 2, tn), lambda j: (0, 0, j))],
        out_specs=[pl.BlockSpec((m, tn), lambda j: (0, j)),
                   pl.BlockSpec((bsz, 2, tn), lambda j: (0, 0, j))],
        out_shape=[jax.ShapeDtypeStruct((m, d_ff), BF16), jax.ShapeDtypeStruct((bsz, 2, d_ff), F32)],
        compiler_params=_params(32, 1),
        name="up_proj_conv_gate_sample",
    )(h, w_up, w_up, conv_w, conv_b.reshape(1, d_ff), state)


class _Weights:
    def __init__(self, depth):
        self.w_in = [None] * depth
        self.w_out = [None] * depth
        self.w_up = [None] * depth
        self.w_down = [None] * depth


def _trunk(x, mods, boff, p, wb, raw, cache, state):
    bsz, seq, d = x.shape
    depth = p["w_in"].shape[0]
    m = bsz * seq
    d_a, d_b = p["g_out_a"].shape[1], p["g_out_b"].shape[1]
    window, n_kv, hd = cache[0].shape[2:]
    kvw = n_kv * hd
    sample = raw is None
    n_blk = min(seq, p["sgu_w"].shape[2])
    ks, vs, convs, sgus = [], [], [], []
    h = _prenorm(x, p["g_pre_mix"][0], mods[0], boff, 1, 0)
    for l in range(depth):
        zm, zkv = _in_proj(h.reshape(m, d), wb.w_in[l], 2 * kvw)
        zm3, zkv3 = zm.reshape(bsz, seq, 3 * d_b), zkv.reshape(bsz, seq, 2 * kvw)
        kept = seq if sample else window
        tail = zkv3[:, seq - kept:, :]
        ks.append(tail[:, :, :kvw].reshape(bsz, kept, n_kv, hd))
        vs.append(tail[:, :, kvw:].reshape(bsz, kept, n_kv, hd))
        if sample:
            a_n, vn = _sgu(zm, p["sgu_w"][l], p["sgu_b"][l], p["g_out_a"][l], n_blk, True)
            b_n = _attn_sample(zm3, zkv3, cache[0][l], cache[1][l], p["attn_sink"][l], p["g_out_b"][l], n_kv, hd)
            sgus.append(vn.reshape(bsz, seq, d_a))
        else:
            (a_n,) = _sgu(zm, p["sgu_w"][l], p["sgu_b"][l], p["g_out_a"][l], n_blk, False)
            b_n = _attn_prompt(zm3, zkv3, p["attn_sink"][l], p["g_out_b"][l], window, n_kv, hd)
        up_cast = ((raw[2], l, 1),) if not sample and l == 0 else ()
        mix, done = _proj([a_n, b_n.reshape(m, d_b)], wb.w_out[l], OUT_TM,
                          OUT_TN_HOST if up_cast else OUT_TN, 48, "out_proj", up_cast)
        if up_cast:
            (wb.w_up[l],) = done
        x, h = _resid(x, mix, p["g_post_mix"][l], mods[l], boff, 2, (p["g_pre_ffn"][l], mods[l], 4, 3))
        if sample:
            act, new_conv = _up_proj_sample(h.reshape(m, d), wb.w_up[l], p["conv_w"][l], p["conv_b"][l], state[l])
        else:
            last_layer = l + 1 == depth
            casts = ((raw[3], l, 1),) + (() if last_layer else ((raw[2], l + 1, raw[2].shape[2] // (2 * UP_TN)),))
            act, new_conv, done = _up_proj_prompt(
                h.reshape(m, d), wb.w_up[l], p["conv_w"][l], p["conv_b"][l], bsz, casts)
            wb.w_down[l] = done[0]
            if not last_layer:
                wb.w_up[l + 1] = done[1]
        convs.append(new_conv)
        nxt_casts = () if sample or l + 1 == depth else ((raw[0], l + 1, 1), (raw[1], l + 1, 1))
        f, done = _proj([act], wb.w_down[l], DOWN_TM, DOWN_TN, 56, "down_proj", nxt_casts)
        if nxt_casts:
            wb.w_in[l + 1], wb.w_out[l + 1] = done
        nxt = (p["g_pre_mix"][l + 1], mods[l + 1], 1, 0) if l + 1 < depth else None
        x, h = _resid(x, f, p["g_post_ffn"][l], mods[l], boff, 5, nxt)
    sgu = jnp.stack(sgus) if sample else None
    return x, jnp.stack(ks), jnp.stack(vs), jnp.stack(convs), sgu


def kernel(x_prompt, x_sample, c_prompt, c_sample, cache_k, cache_v, state_conv, w_mod, b_mod, g_pre_mix, g_post_mix, w_in, sgu_w, sgu_b, attn_sink, g_out_a, g_out_b, w_out, g_pre_ffn, g_post_ffn, w_up, conv_w, conv_b, w_down):
    depth, d = g_pre_mix.shape
    nb_p = c_prompt.shape[0]
    c_all = jnp.concatenate([c_prompt, c_sample], axis=0)
    mod = _modulation(c_all, w_mod, b_mod).reshape(depth, c_all.shape[0], N_MOD, 1, d)
    mods = [mod[l] for l in range(depth)]
    p = dict(g_pre_mix=g_pre_mix, g_post_mix=g_post_mix, w_in=w_in, sgu_w=sgu_w, sgu_b=sgu_b,
             attn_sink=attn_sink, g_out_a=g_out_a, g_out_b=g_out_b, g_pre_ffn=g_pre_ffn,
             g_post_ffn=g_post_ffn, conv_w=conv_w, conv_b=conv_b)
    wb = _Weights(depth)
    wb.w_in[0] = w_in[0].astype(BF16)
    wb.w_out[0] = w_out[0].astype(BF16)
    cache = (cache_k, cache_v)
    y_p, k_p, v_p, conv_p, _ = _trunk(x_prompt, mods, 0, p, wb, (w_in, w_out, w_up, w_down), cache, None)
    y_s, k_s, v_s, conv_s, sgu_s = _trunk(x_sample, mods, nb_p, p, wb, None, cache, state_conv)
    return (y_p, y_s, k_p, v_p, k_s, v_s, conv_p, conv_s, sgu_s)
```

```python
import functools

import jax
import jax.numpy as jnp
from jax import lax
from jax.experimental import pallas as pl
from jax.experimental.pallas import tpu as pltpu

F32 = jnp.float32
BF16 = jnp.bfloat16

CHUNK = 64
EPS = 1e-6
NEG = -1e30
N_MOD = 6
LOG2E = 1.4426950408889634
LANES = 128
BF16_ROWS = 16

MOD_TN = 512
NORM_TS = 512
RESID_TS = 512
PROJ_TM = 2048
OUT_TM, OUT_TN = 1024, 1024
OUT_TN_HOST = 512
UP_TN = 256
UP_RC = 1024
DOWN_TM, DOWN_TN = 512, 512
SGU_ROWS = 1024
ATTN_TQ = 128
MIB = 1024 * 1024


def _tile(full, pref):
    t = min(full, pref)
    assert full % t == 0, (full, pref)
    return t


def _params(vmem_mib, n_axes):
    return pltpu.CompilerParams(dimension_semantics=("arbitrary",) * n_axes,
                                vmem_limit_bytes=vmem_mib * MIB)


def _rms(x):
    return x * lax.rsqrt(jnp.mean(x * x, axis=-1, keepdims=True) + EPS)


def _with_casts(body, n_in, n_out, n_cast):
    if not n_cast:
        return body

    def kernel(*refs):
        ins = refs[:n_in]
        cast_in = refs[n_in:n_in + n_cast]
        outs = refs[n_in + n_cast:n_in + n_cast + n_out]
        cast_out = refs[n_in + n_cast + n_out:n_in + 2 * n_cast + n_out]
        for src, dst in zip(cast_in, cast_out):
            dst[...] = src[...].astype(BF16)
        body(*ins, *outs, *refs[n_in + 2 * n_cast + n_out:])

    return kernel


def _cast_specs(casts, n_steps, step_of):
    in_specs, out_specs, shapes = [], [], []
    for w, l, cb in casts:
        _, rows, cols = w.shape
        assert n_steps % cb == 0 and cols % cb == 0
        slab, width = rows // (n_steps // cb), cols // cb
        assert slab * n_steps == rows * cb and slab % BF16_ROWS == 0 and (width % LANES == 0 or cb == 1), (
            w.shape, n_steps, cb)
        in_specs.append(pl.BlockSpec((None, slab, width),
                                     lambda *g, l=l, cb=cb: (l, step_of(*g) // cb, step_of(*g) % cb)))
        out_specs.append(pl.BlockSpec((slab, width), lambda *g, cb=cb: (step_of(*g) // cb, step_of(*g) % cb)))
        shapes.append(jax.ShapeDtypeStruct((rows, cols), BF16))
    return in_specs, out_specs, shapes


def _mod_kernel(c_ref, w_ref, b_ref, o_ref):
    act = jax.nn.silu(c_ref[...]).astype(BF16)
    o_ref[...] = jnp.dot(act, w_ref[...].astype(BF16), preferred_element_type=F32) + b_ref[...]


def _modulation(c, w_mod, b_mod):
    depth, d, n = w_mod.shape
    rows = c.shape[0]
    tn = _tile(n, MOD_TN)
    return pl.pallas_call(
        _mod_kernel,
        grid=(depth, n // tn),
        in_specs=[pl.BlockSpec((rows, d), lambda l, j: (0, 0)),
                  pl.BlockSpec((None, d, tn), lambda l, j: (l, 0, j)),
                  pl.BlockSpec((None, 1, tn), lambda l, j: (l, 0, j))],
        out_specs=pl.BlockSpec((None, rows, tn), lambda l, j: (l, 0, j)),
        out_shape=jax.ShapeDtypeStruct((depth, rows, n), F32),
        compiler_params=_params(40, 2),
        name="modulation",
    )(c, w_mod, b_mod.reshape(depth, 1, n))


def _prenorm_kernel(x_ref, g_ref, sc_ref, sh_ref, h_ref):
    y = _rms(x_ref[...]) * g_ref[...]
    h_ref[...] = (y * (1 + sc_ref[...]) + sh_ref[...]).astype(BF16)


def _mod_spec(boff, j, d):
    return pl.BlockSpec((None, None, 1, d), lambda b, t: (boff + b, j, 0, 0))


def _prenorm(x, g, mod, boff, j_sc, j_sh):
    bsz, s, d = x.shape
    ts = _tile(s, NORM_TS)
    return pl.pallas_call(
        _prenorm_kernel,
        grid=(bsz, s // ts),
        in_specs=[pl.BlockSpec((None, ts, d), lambda b, t: (b, t, 0)),
                  pl.BlockSpec((1, d), lambda b, t: (0, 0)),
                  _mod_spec(boff, j_sc, d), _mod_spec(boff, j_sh, d)],
        out_specs=pl.BlockSpec((None, ts, d), lambda b, t: (b, t, 0)),
        out_shape=jax.ShapeDtypeStruct((bsz, s, d), BF16),
        compiler_params=_params(32, 2),
        name="prenorm",
    )(x, g.reshape(1, d), mod, mod)


def _resid_kernel(x_ref, m_ref, gt_ref, gpost_ref, *rest, with_next):
    xn = x_ref[...] + gt_ref[...] * (_rms(m_ref[...].astype(F32)) * gpost_ref[...])
    if with_next:
        gpre_ref, sc_ref, sh_ref, xo_ref, h_ref = rest
        y = _rms(xn) * gpre_ref[...]
        h_ref[...] = (y * (1 + sc_ref[...]) + sh_ref[...]).astype(BF16)
    else:
        (xo_ref,) = rest
    xo_ref[...] = xn


def _resid(x, m, g_post, mod, boff, j_gt, nxt=None):
    bsz, s, d = x.shape
    ts = _tile(s, RESID_TS)
    row = pl.BlockSpec((None, ts, d), lambda b, t: (b, t, 0))
    vec = pl.BlockSpec((1, d), lambda b, t: (0, 0))
    in_specs = [row, row, _mod_spec(boff, j_gt, d), vec]
    args = [x, m.reshape(bsz, s, d), mod, g_post.reshape(1, d)]
    out_specs = [row]
    out_shape = [jax.ShapeDtypeStruct((bsz, s, d), F32)]
    if nxt is not None:
        g_pre, mod_next, j_sc, j_sh = nxt
        in_specs += [vec, _mod_spec(boff, j_sc, d), _mod_spec(boff, j_sh, d)]
        args += [g_pre.reshape(1, d), mod_next, mod_next]
        out_specs.append(row)
        out_shape.append(jax.ShapeDtypeStruct((bsz, s, d), BF16))
    out = pl.pallas_call(
        functools.partial(_resid_kernel, with_next=nxt is not None),
        grid=(bsz, s // ts),
        in_specs=in_specs, out_specs=out_specs, out_shape=out_shape,
        compiler_params=_params(58, 2),
        name="resid_norm",
    )(*args)
    return out if nxt is not None else (out[0], None)


def _in_proj_kernel(a_ref, w_ref, zm_ref, zkv_ref):
    r = jnp.dot(a_ref[...], w_ref[...], preferred_element_type=F32)
    zm_ref[...] = r.astype(BF16)

    @pl.when(pl.program_id(1) == 0)
    def _():
        zkv_ref[...] = r


def _in_proj(a, w, kv_cols):
    m, k = a.shape
    n = w.shape[1]
    tm, tn = _tile(m, PROJ_TM), kv_cols
    assert (n - kv_cols) % tn == 0
    n_main = (n - kv_cols) // tn
    return pl.pallas_call(
        _in_proj_kernel,
        grid=(m // tm, n_main + 1),
        in_specs=[pl.BlockSpec((tm, k), lambda i, j: (i, 0)),
                  pl.BlockSpec((k, tn), lambda i, j: (0, jnp.where(j == 0, n_main, j - 1)))],
        out_specs=[pl.BlockSpec((tm, tn), lambda i, j: (i, jnp.maximum(j - 1, 0))),
                   pl.BlockSpec((tm, tn), lambda i, j: (i, 0))],
        out_shape=[jax.ShapeDtypeStruct((m, n - kv_cols), BF16), jax.ShapeDtypeStruct((m, kv_cols), F32)],
        compiler_params=_params(56, 2),
        name="in_proj",
    )(a, w)


def _mm_kernel(a_ref, w_ref, o_ref):
    o_ref[...] = jnp.dot(a_ref[...], w_ref[...], preferred_element_type=F32).astype(o_ref.dtype)


def _mm2_kernel(a1_ref, a2_ref, w1_ref, w2_ref, o_ref):
    o_ref[...] = (jnp.dot(a1_ref[...], w1_ref[...], preferred_element_type=F32)
                  + jnp.dot(a2_ref[...], w2_ref[...], preferred_element_type=F32)).astype(o_ref.dtype)


def _proj(acts, w, tm, tn, vmem_mib, name, casts=()):
    m, k = acts[0].shape
    assert all(a.shape == (m, k) for a in acts) and w.shape[0] == k * len(acts)
    n = w.shape[1]
    tm, tn = _tile(m, tm), _tile(n, tn)
    nj = n // tn
    n_steps = (m // tm) * nj
    cast_in, cast_out, cast_shapes = _cast_specs(casts, n_steps, lambda i, j: i * nj + j)
    body = _mm_kernel if len(acts) == 1 else _mm2_kernel
    out = pl.pallas_call(
        _with_casts(body, 2 * len(acts), 1, len(casts)),
        grid=(m // tm, nj),
        in_specs=([pl.BlockSpec((tm, k), lambda i, j: (i, 0))] * len(acts)
                  + [pl.BlockSpec((k, tn), lambda i, j, r=r: (r, j)) for r in range(len(acts))]
                  + cast_in),
        out_specs=[pl.BlockSpec((tm, tn), lambda i, j: (i, j))] + cast_out,
        out_shape=[jax.ShapeDtypeStruct((m, n), BF16)] + cast_shapes,
        compiler_params=_params(vmem_mib, 2),
        name=name,
    )(*acts, *([w] * len(acts)), *[c[0] for c in casts])
    return out[0], out[1:]


def _sgu_kernel(u_ref, v_ref, w_ref, bt_ref, g_ref, o_ref, *rest, n, groups, with_vn):
    if with_vn:
        vn_ref, acc_ref = rest
    else:
        (acc_ref,) = rest
    rows, d_a = u_ref.shape
    gd = d_a // groups
    qi = lax.broadcasted_iota(jnp.int32, (n, n), 0) // CHUNK
    kj = lax.broadcasted_iota(jnp.int32, (n, n), 1) // CHUNK
    causal = kj <= qi
    sq = [jnp.zeros((n, gd), F32) for _ in range(rows // n)]
    for g in range(groups):
        wg = jnp.where(causal, w_ref[g], 0.0).astype(BF16)
        bias = bt_ref[:, g:g + 1]
        cols = slice(g * gd, (g + 1) * gd)
        for sb in range(rows // n):
            rs = slice(sb * n, (sb + 1) * n)
            vn = _rms(v_ref[rs, cols].astype(F32))
            if with_vn:
                vn_ref[rs, cols] = vn
            s = jnp.dot(wg, vn.astype(BF16), preferred_element_type=F32) + bias
            out = u_ref[rs, cols].astype(F32) * s
            acc_ref[rs, cols] = out
            sq[sb] = sq[sb] + out * out
    for sb in range(rows // n):
        rs = slice(sb * n, (sb + 1) * n)
        inv = lax.rsqrt(jnp.sum(sq[sb], axis=-1, keepdims=True) / d_a + EPS)
        o_ref[rs, :] = (acc_ref[rs, :] * inv * g_ref[...]).astype(BF16)


def _sgu(z, w_s, b_s, g_out, n, with_vn):
    m = z.shape[0]
    d_a = g_out.shape[0]
    groups = w_s.shape[0]
    rows = _tile(m, max(n, SGU_ROWS))
    out_specs = [pl.BlockSpec((rows, d_a), lambda i: (i, 0))]
    out_shape = [jax.ShapeDtypeStruct((m, d_a), BF16)]
    if with_vn:
        out_specs.append(pl.BlockSpec((rows, d_a), lambda i: (i, 0)))
        out_shape.append(jax.ShapeDtypeStruct((m, d_a), F32))
    return pl.pallas_call(
        functools.partial(_sgu_kernel, n=n, groups=groups, with_vn=with_vn),
        grid=(m // rows,),
        in_specs=[pl.BlockSpec((rows, d_a), lambda i: (i, 0)),
                  pl.BlockSpec((rows, d_a), lambda i: (i, 1)),
                  pl.BlockSpec((groups, n, n), lambda i: (0, 0, 0)),
                  pl.BlockSpec((n, groups), lambda i: (0, 0)),
                  pl.BlockSpec((1, d_a), lambda i: (0, 0))],
        out_specs=out_specs, out_shape=out_shape,
        scratch_shapes=[pltpu.VMEM((rows, d_a), F32)],
        compiler_params=_params(48, 1),
        name="sgu_mixer",
    )(z, z, w_s[:, :n, :n], b_s[:, :n].T, g_out.reshape(1, d_a))


def _attend(q_ref, kw, vw, bias, sink_ref, g_ref, o_ref, acc_ref, *, n_kv, hd, phased):
    tq, d_b = q_ref.shape
    span = kw.shape[0]
    rep = d_b // (n_kv * hd)
    assert 2 * hd == LANES and rep % 2 == 0
    kscale = hd ** -0.5 * LOG2E
    zeros = jnp.zeros((span, hd), BF16)
    ones = jnp.ones((span, hd), BF16)
    first = lax.broadcasted_iota(jnp.int32, (tq, LANES), 1) < hd
    nt = (((1,), (1,)), ((), ()))

    def group_operands(g):
        kg = (kw[:, g * hd:(g + 1) * hd] * kscale).astype(BF16)
        vg = vw[:, g * hd:(g + 1) * hd].astype(BF16)
        keys = (jnp.concatenate([kg, zeros], axis=1), jnp.concatenate([zeros, kg], axis=1))
        vals = (jnp.concatenate([vg, zeros, ones, zeros], axis=1),
                jnp.concatenate([zeros, vg, zeros, ones], axis=1))
        return keys, vals

    def logits(h0, key):
        s = lax.dot_general(q_ref[:, h0 * hd:(h0 + 2) * hd], key, nt, preferred_element_type=F32)
        return s if bias is None else s + bias

    def exps(s, h):
        sk = sink_ref[h] * LOG2E
        mx = jnp.maximum(jnp.max(s, axis=-1, keepdims=True), sk)
        return jnp.exp2(s - mx).astype(BF16), jnp.exp2(sk - mx)

    def finish(h0, o0, o1, sink0, sink1):
        out = o0 + o1
        den = out[:, LANES:] + jnp.where(first, sink0, sink1)
        acc_ref[:, h0 * hd:(h0 + 2) * hd] = out[:, :LANES] / den

    pairs = [(g, g * rep + 2 * p) for g in range(n_kv) for p in range(rep // 2)]
    if phased:
        ops = [group_operands(g) for g in range(n_kv)]
        s_all = [[logits(h0, ops[g][0][i]) for i in range(2)] for g, h0 in pairs]
        e_all = [[exps(s[i], h0 + i) for i in range(2)] for s, (g, h0) in zip(s_all, pairs)]
        o_all = [[jnp.dot(e[i][0], ops[g][1][i], preferred_element_type=F32) for i in range(2)]
                 for e, (g, h0) in zip(e_all, pairs)]
        for o, e, (g, h0) in zip(o_all, e_all, pairs):
            finish(h0, o[0], o[1], e[0][1], e[1][1])
    else:
        for g, h0 in pairs:
            if h0 == g * rep:
                keys, vals = group_operands(g)
            e = [exps(logits(h0, keys[i]), h0 + i) for i in range(2)]
            o = [jnp.dot(e[i][0], vals[i], preferred_element_type=F32) for i in range(2)]
            finish(h0, o[0], o[1], e[0][1], e[1][1])
    o_ref[...] = (_rms(acc_ref[...]) * g_ref[...]).astype(BF16)


def _attn_prompt_kernel(q_ref, k_ref, v_ref, sink_ref, g_ref, o_ref, acc_ref, *, window, n_kv, hd):
    tq = q_ref.shape[0]
    span = tq + window
    q0 = pl.program_id(1) * tq
    start = pl.multiple_of(jnp.maximum(q0 - window, 0), CHUNK)
    kw = k_ref[pl.ds(start, span), :]
    vw = v_ref[pl.ds(start, span), :]
    qc = (q0 + lax.broadcasted_iota(jnp.int32, (tq, span), 0)) // CHUNK
    kc = (start + lax.broadcasted_iota(jnp.int32, (tq, span), 1)) // CHUNK
    bias = jnp.where(kc <= qc, jnp.where(kc >= qc - window // CHUNK, 0.0, NEG), NEG)
    _attend(q_ref, kw, vw, bias, sink_ref, g_ref, o_ref, acc_ref, n_kv=n_kv, hd=hd, phased=False)


def _attn_prompt(zm3, zkv3, sink, g_out, window, n_kv, hd):
    bsz, s, _ = zm3.shape
    d_b = g_out.shape[0]
    kvw = n_kv * hd
    tq = _tile(s, ATTN_TQ)
    assert tq % CHUNK == 0 and window % CHUNK == 0 and tq + window <= s
    return pl.pallas_call(
        functools.partial(_attn_prompt_kernel, window=window, n_kv=n_kv, hd=hd),
        grid=(bsz, s // tq),
        in_specs=[pl.BlockSpec((None, tq, d_b), lambda b, t: (b, t, 2)),
                  pl.BlockSpec((None, s, kvw), lambda b, t: (b, 0, 0)),
                  pl.BlockSpec((None, s, kvw), lambda b, t: (b, 0, 1)),
                  pl.BlockSpec(memory_space=pltpu.SMEM),
                  pl.BlockSpec((1, d_b), lambda b, t: (0, 0))],
        out_specs=pl.BlockSpec((None, tq, d_b), lambda b, t: (b, t, 0)),
        out_shape=jax.ShapeDtypeStruct((bsz, s, d_b), BF16),
        scratch_shapes=[pltpu.VMEM((tq, d_b), F32)],
        compiler_params=_params(40, 2),
        name="attn_prompt",
    )(zm3, zkv3, zkv3, sink, g_out.reshape(1, d_b))


def _attn_sample_kernel(q_ref, kn_ref, vn_ref, ck_ref, cv_ref, sink_ref, g_ref, o_ref, acc_ref, *, n_kv, hd):
    kw = jnp.concatenate([ck_ref[...], kn_ref[...]], axis=0)
    vw = jnp.concatenate([cv_ref[...], vn_ref[...]], axis=0)
    _attend(q_ref, kw, vw, None, sink_ref, g_ref, o_ref, acc_ref, n_kv=n_kv, hd=hd, phased=True)


def _attn_sample(zm3, zkv3, ck, cv, sink, g_out, n_kv, hd):
    bsz, t, _ = zm3.shape
    d_b = g_out.shape[0]
    kvw = n_kv * hd
    win = ck.shape[1]
    return pl.pallas_call(
        functools.partial(_attn_sample_kernel, n_kv=n_kv, hd=hd),
        grid=(bsz,),
        in_specs=[pl.BlockSpec((None, t, d_b), lambda b: (b, 0, 2)),
                  pl.BlockSpec((None, t, kvw), lambda b: (b, 0, 0)),
                  pl.BlockSpec((None, t, kvw), lambda b: (b, 0, 1)),
                  pl.BlockSpec((None, win, kvw), lambda b: (b, 0, 0)),
                  pl.BlockSpec((None, win, kvw), lambda b: (b, 0, 0)),
                  pl.BlockSpec(memory_space=pltpu.SMEM),
                  pl.BlockSpec((1, d_b), lambda b: (0, 0))],
        out_specs=pl.BlockSpec((None, t, d_b), lambda b: (b, 0, 0)),
        out_shape=jax.ShapeDtypeStruct((bsz, t, d_b), BF16),
        scratch_shapes=[pltpu.VMEM((t, d_b), F32)],
        compiler_params=_params(32, 1),
        name="attn_sample",
    )(zm3, zkv3, zkv3, ck.reshape(bsz, win, kvw), cv.reshape(bsz, win, kvw), sink, g_out.reshape(1, d_b))


def _conv_silu(a, prev, cw, cb):
    rows = a.shape[0]
    row = lax.broadcasted_iota(jnp.int32, a.shape, 0)
    a1 = jnp.where(row == 0, prev[1:2], pltpu.roll(a, 1, 0))
    a2 = jnp.where(row == 0, prev[0:1], jnp.where(row == 1, prev[1:2], pltpu.roll(a, 2, 0)))
    conv = cw[0:1] * a2 + cw[1:2] * a1 + cw[2:3] * a + cb
    return jax.nn.silu(conv), a[rows - 2:rows]


def _up_prompt_kernel(h_ref, wa_ref, wb_ref, cw_ref, cb_ref, act_ref, nc_ref):
    tm = h_ref.shape[0]
    tn = wa_ref.shape[1]
    cols = pl.ds(pl.multiple_of(pl.program_id(1) * tn, tn), tn)
    cw, cb = cw_ref[:, cols], cb_ref[:, cols]
    last = jnp.zeros((2, tn), F32)
    rc = _tile(tm, UP_RC)
    for c in range(tm // rc):
        rs = slice(c * rc, (c + 1) * rc)
        h = h_ref[rs, :]
        a = jnp.dot(h, wa_ref[...], preferred_element_type=F32)
        gate, last = _conv_silu(a, last, cw, cb)
        b = jnp.dot(h, wb_ref[...], preferred_element_type=F32)
        act_ref[rs, :] = (gate * b).astype(BF16)
    nc_ref[:, cols] = last


def _up_proj_prompt(h, w_up, conv_w, conv_b, bsz, casts=()):
    m, d = h.shape
    d_ff = conv_w.shape[1]
    tm = m // bsz
    tn = _tile(d_ff, UP_TN)
    nj = d_ff // tn
    cast_in, cast_out, cast_shapes = _cast_specs(casts, bsz * nj, lambda i, j: i * nj + j)
    out = pl.pallas_call(
        _with_casts(_up_prompt_kernel, 5, 2, len(casts)),
        grid=(bsz, nj),
        in_specs=[pl.BlockSpec((tm, d), lambda i, j: (i, 0)),
                  pl.BlockSpec((d, tn), lambda i, j: (0, j)),
                  pl.BlockSpec((d, tn), lambda i, j: (0, nj + j)),
                  pl.BlockSpec((conv_w.shape[0], d_ff), lambda i, j: (0, 0)),
                  pl.BlockSpec((1, d_ff), lambda i, j: (0, 0))] + cast_in,
        out_specs=[pl.BlockSpec((tm, tn), lambda i, j: (i, j)),
                   pl.BlockSpec((None, 2, d_ff), lambda i, j: (i, 0, 0))] + cast_out,
        out_shape=[jax.ShapeDtypeStruct((m, d_ff), BF16),
                   jax.ShapeDtypeStruct((bsz, 2, d_ff), F32)] + cast_shapes,
        compiler_params=_params(56, 2),
        name="up_proj_conv_gate",
    )(h, w_up, w_up, conv_w, conv_b.reshape(1, d_ff), *[c[0] for c in casts])
    return out[0], out[1], out[2:]


def _up_sample_kernel(h_ref, wa_ref, wb_ref, cw_ref, cb_ref, st_ref, act_ref, nc_ref):
    h = h_ref[...]
    a = jnp.dot(h, wa_ref[...], preferred_element_type=F32)
    b = jnp.dot(h, wb_ref[...], preferred_element_type=F32)
    cw, cb = cw_ref[...], cb_ref[...]
    pieces = st_ref.shape[0]
    t = h.shape[0] // pieces
    for p in range(pieces):
        rs = slice(p * t, (p + 1) * t)
        gate, nc_ref[p] = _conv_silu(a[rs], st_ref[p], cw, cb)
        act_ref[rs, :] = (gate * b[rs]).astype(BF16)


def _up_proj_sample(h, w_up, conv_w, conv_b, state):
    m, d = h.shape
    bsz = state.shape[0]
    d_ff = conv_w.shape[1]
    tn = _tile(d_ff, UP_TN)
    nj = d_ff // tn
    return pl.pallas_call(
        _up_sample_kernel,
        grid=(nj,),
        in_specs=[pl.BlockSpec((m, d), lambda j: (0, 0)),
                  pl.BlockSpec((d, tn), lambda j: (0, j)),
                  pl.BlockSpec((d, tn), lambda j: (0, nj + j)),
                  pl.BlockSpec((conv_w.shape[0], tn), lambda j: (0, j)),
                  pl.BlockSpec((1, tn), lambda j: (0, j)),
                  pl.BlockSpec((bsz, 2, tn), lambda j: (0, 0, j))],
        out_specs=[pl.BlockSpec((m, tn), lambda j: (0, j)),
                   pl.BlockSpec((bsz, 2, tn), lambda j: (0, 0, j))],
        out_shape=[jax.ShapeDtypeStruct((m, d_ff), BF16), jax.ShapeDtypeStruct((bsz, 2, d_ff), F32)],
        compiler_params=_params(32, 1),
        name="up_proj_conv_gate_sample",
    )(h, w_up, w_up, conv_w, conv_b.reshape(1, d_ff), state)


class _Weights:
    def __init__(self, depth):
        self.w_in = [None] * depth
        self.w_out = [None] * depth
        self.w_up = [None] * depth
        self.w_down = [None] * depth


def _trunk(x, mods, boff, p, wb, raw, cache, state):
    bsz, seq, d = x.shape
    depth = p["w_in"].shape[0]
    m = bsz * seq
    d_a, d_b = p["g_out_a"].shape[1], p["g_out_b"].shape[1]
    window, n_kv, hd = cache[0].shape[2:]
    kvw = n_kv * hd
    sample = raw is None
    n_blk = min(seq, p["sgu_w"].shape[2])
    ks, vs, convs, sgus = [], [], [], []
    h = _prenorm(x, p["g_pre_mix"][0], mods[0], boff, 1, 0)
    for l in range(depth):
        zm, zkv = _in_proj(h.reshape(m, d), wb.w_in[l], 2 * kvw)
        zm3, zkv3 = zm.reshape(bsz, seq, 3 * d_b), zkv.reshape(bsz, seq, 2 * kvw)
        kept = seq if sample else window
        tail = zkv3[:, seq - kept:, :]
        ks.append(tail[:, :, :kvw].reshape(bsz, kept, n_kv, hd))
        vs.append(tail[:, :, kvw:].reshape(bsz, kept, n_kv, hd))
        if sample:
            a_n, vn = _sgu(zm, p["sgu_w"][l], p["sgu_b"][l], p["g_out_a"][l], n_blk, True)
            b_n = _attn_sample(zm3, zkv3, cache[0][l], cache[1][l], p["attn_sink"][l], p["g_out_b"][l], n_kv, hd)
            sgus.append(vn.reshape(bsz, seq, d_a))
        else:
            (a_n,) = _sgu(zm, p["sgu_w"][l], p["sgu_b"][l], p["g_out_a"][l], n_blk, False)
            b_n = _attn_prompt(zm3, zkv3, p["attn_sink"][l], p["g_out_b"][l], window, n_kv, hd)
        up_cast = ((raw[2], l, 1),) if not sample and l == 0 else ()
        mix, done = _proj([a_n, b_n.reshape(m, d_b)], wb.w_out[l], OUT_TM,
                          OUT_TN_HOST if up_cast else OUT_TN, 48, "out_proj", up_cast)
        if up_cast:
            (wb.w_up[l],) = done
        x, h = _resid(x, mix, p["g_post_mix"][l], mods[l], boff, 2, (p["g_pre_ffn"][l], mods[l], 4, 3))
        if sample:
            act, new_conv = _up_proj_sample(h.reshape(m, d), wb.w_up[l], p["conv_w"][l], p["conv_b"][l], state[l])
        else:
            last_layer = l + 1 == depth
            casts = ((raw[3], l, 1),) + (() if last_layer else ((raw[2], l + 1, raw[2].shape[2] // (2 * UP_TN)),))
            act, new_conv, done = _up_proj_prompt(
                h.reshape(m, d), wb.w_up[l], p["conv_w"][l], p["conv_b"][l], bsz, casts)
            wb.w_down[l] = done[0]
            if not last_layer:
                wb.w_up[l + 1] = done[1]
        convs.append(new_conv)
        nxt_casts = () if sample or l + 1 == depth else ((raw[0], l + 1, 1), (raw[1], l + 1, 1))
        f, done = _proj([act], wb.w_down[l], DOWN_TM, DOWN_TN, 56, "down_proj", nxt_casts)
        if nxt_casts:
            wb.w_in[l + 1], wb.w_out[l + 1] = done
        nxt = (p["g_pre_mix"][l + 1], mods[l + 1], 1, 0) if l + 1 < depth else None
        x, h = _resid(x, f, p["g_post_ffn"][l], mods[l], boff, 5, nxt)
    sgu = jnp.stack(sgus) if sample else None
    return x, jnp.stack(ks), jnp.stack(vs), jnp.stack(convs), sgu


def kernel(x_prompt, x_sample, c_prompt, c_sample, cache_k, cache_v, state_conv, w_mod, b_mod, g_pre_mix, g_post_mix, w_in, sgu_w, sgu_b, attn_sink, g_out_a, g_out_b, w_out, g_pre_ffn, g_post_ffn, w_up, conv_w, conv_b, w_down):
    depth, d = g_pre_mix.shape
    nb_p = c_prompt.shape[0]
    c_all = jnp.concatenate([c_prompt, c_sample], axis=0)
    mod = _modulation(c_all, w_mod, b_mod).reshape(depth, c_all.shape[0], N_MOD, 1, d)
    mods = [mod[l] for l in range(depth)]
    p = dict(g_pre_mix=g_pre_mix, g_post_mix=g_post_mix, w_in=w_in, sgu_w=sgu_w, sgu_b=sgu_b,
             attn_sink=attn_sink, g_out_a=g_out_a, g_out_b=g_out_b, g_pre_ffn=g_pre_ffn,
             g_post_ffn=g_post_ffn, conv_w=conv_w, conv_b=conv_b)
    wb = _Weights(depth)
    wb.w_in[0] = w_in[0].astype(BF16)
    wb.w_out[0] = w_out[0].astype(BF16)
    cache = (cache_k, cache_v)
    y_p, k_p, v_p, conv_p, _ = _trunk(x_prompt, mods, 0, p, wb, (w_in, w_out, w_up, w_down), cache, None)
    y_s, k_s, v_s, conv_s, sgu_s = _trunk(x_sample, mods, nb_p, p, wb, None, cache, state_conv)
    return (y_p, y_s, k_p, v_p, k_s, v_s, conv_p, conv_s, sgu_s)
```
